```python
import math
import jax, jax.numpy as jnp
from jax import lax
import numpy as np

D_MODEL = 2048
BATCH = 1
SEQ = 16384
DEPTH = 4

N_MIXERS = 3
N_FOX_LAYERS = (DEPTH + 2) // 3
N_RWKV_LAYERS = (DEPTH + 1) // 3
N_S5_LAYERS = DEPTH // 3

RMS_EPS = 1e-6
D_FF = ((int(2 * 4 * D_MODEL / 3) + 255) // 256) * 256

FOX_HEAD_DIM = 128
FOX_HEADS = D_MODEL // FOX_HEAD_DIM
FOX_BLOCK = 128
FOX_IN = 4 * D_MODEL + FOX_HEADS

RWKV_HEAD_DIM = 64
RWKV_HEADS = D_MODEL // RWKV_HEAD_DIM
RWKV_DECAY_LORA = max(32, int(round(1.8 * D_MODEL ** 0.5 / 32)) * 32)
RWKV_A_LORA = max(32, int(round(1.8 * D_MODEL ** 0.5 / 32)) * 32)
RWKV_GATE_LORA = max(32, int(round(0.6 * D_MODEL ** 0.8 / 32)) * 32)
RWKV_LN_EPS = 64e-5
RWKV_NORM_EPS = 1e-12

S5_GROUP = 16
S5_GROUPS = D_MODEL // S5_GROUP
S5_STATE = 64
S5_CHUNK = 128
S5_DT_MIN = 1e-3
S5_DT_MAX = 1e-1
S5_MAX_RE = -1e-4

kernel_name = "fox_rwkv7_s5_macaron_hybrid"


def rms_norm(x, g):
    xf = x.astype(jnp.float32)
    y = xf * lax.rsqrt(jnp.mean(xf * xf, axis=-1, keepdims=True) + RMS_EPS)
    return (y * g.astype(jnp.float32)).astype(x.dtype)


def swiglu(h, w_up, w_down):
    gate, up = jnp.split(h @ w_up, 2, axis=-1)
    return (jax.nn.silu(gate) * up) @ w_down


def fox_mixer(h, w_in, b_f, qk_gain, w_out):
    b, s, d = h.shape
    proj = h @ w_in
    q = proj[..., 0 * d:1 * d].reshape(b, s, FOX_HEADS, FOX_HEAD_DIM)
    k = proj[..., 1 * d:2 * d].reshape(b, s, FOX_HEADS, FOX_HEAD_DIM)
    v = proj[..., 2 * d:3 * d].reshape(b, s, FOX_HEADS, FOX_HEAD_DIM)
    g = proj[..., 3 * d:4 * d]
    f_logit = proj[..., 4 * d:]
    q = rms_norm(q, qk_gain[0]).transpose(0, 2, 1, 3)
    k = rms_norm(k, qk_gain[1]).transpose(0, 2, 1, 3)
    v = v.transpose(0, 2, 1, 3)
    log_f = jax.nn.log_sigmoid(f_logit.astype(jnp.float32) + b_f.astype(jnp.float32))
    c = jnp.cumsum(log_f, axis=1).transpose(0, 2, 1)
    nb = s // FOX_BLOCK
    q_blocks = q.reshape(b, FOX_HEADS, nb, FOX_BLOCK, FOX_HEAD_DIM).transpose(2, 0, 1, 3, 4)
    c_blocks = c.reshape(b, FOX_HEADS, nb, FOX_BLOCK).transpose(2, 0, 1, 3)
    starts = jnp.arange(nb, dtype=jnp.int32) * FOX_BLOCK
    kpos = jnp.arange(s, dtype=jnp.int32)
    scale = FOX_HEAD_DIM ** -0.5

    def one_block(args):
        qb, cb, st = args
        logits = jnp.einsum('bhqd,bhkd->bhqk', qb, k).astype(jnp.float32) * scale
        logits = logits + cb[..., :, None] - c[:, :, None, :]
        qpos = st + jnp.arange(FOX_BLOCK, dtype=jnp.int32)
        logits = jnp.where(kpos[None, :] <= qpos[:, None], logits, -jnp.inf)
        p = jax.nn.softmax(logits, axis=-1).astype(v.dtype)
        return jnp.einsum('bhqk,bhkd->bhqd', p, v)

    o = lax.map(one_block, (q_blocks, c_blocks, starts))
    o = o.transpose(1, 0, 3, 2, 4).reshape(b, s, d)
    return (o * jax.nn.sigmoid(g)) @ w_out


def rwkv7_mixer(h, mu, w_rkv, w0, w1, w2, a0, a1, a2, g1, g2, k_k, k_a, r_k, ln_w, ln_b, w_out):
    b, s, d = h.shape
    H, N = RWKV_HEADS, RWKV_HEAD_DIM
    f32 = jnp.float32
    xx = jnp.pad(h, ((0, 0), (1, 0), (0, 0)))[:, :-1] - h
    xr, xw, xk, xv, xa, xg = [h + xx * mu[i] for i in range(6)]
    rkv = jnp.einsum('nbsd,nde->nbse', jnp.stack([xr, xk, xv]), w_rkv)
    r, k, v = rkv[0].astype(f32), rkv[1].astype(f32), rkv[2].astype(f32)
    w_log = -jax.nn.softplus(-(w0 + jnp.tanh(xw @ w1) @ w2).astype(f32)) - 0.5
    decay = jnp.exp(-jnp.exp(w_log))
    a = jax.nn.sigmoid((a0 + (xa @ a1) @ a2).astype(f32))
    g = jax.nn.sigmoid(xg @ g1) @ g2
    kk = (k * k_k.astype(f32)).reshape(b, s, H, N)
    kk = kk / jnp.maximum(jnp.sqrt(jnp.sum(kk * kk, axis=-1, keepdims=True)), RWKV_NORM_EPS)
    k = k * (1.0 + (a - 1.0) * k_a.astype(f32))
    heads = lambda t: t.reshape(b, s, H, N)
    r4, k4, v4, w4, a4 = heads(r), heads(k), heads(v), heads(decay), heads(a)
    seq_first = lambda t: jnp.moveaxis(t, 1, 0)
    xs = (seq_first(r4), seq_first(w4), seq_first(k4), seq_first(v4), seq_first(-kk), seq_first(kk * a4))

    def step(state, inp):
        rt, wt, kt, vt, at, bt = inp
        sa = jnp.einsum('bhij,bhj->bhi', state, at)
        state = state * wt[:, :, None, :] + sa[..., None] * bt[:, :, None, :] + vt[..., None] * kt[:, :, None, :]
        return state, jnp.einsum('bhij,bhj->bhi', state, rt)

    state0 = jnp.zeros((b, H, N, N), f32)
    _, y = lax.scan(step, state0, xs)
    y = jnp.moveaxis(y, 0, 1)
    mean = jnp.mean(y, axis=-1, keepdims=True)
    var = jnp.mean((y - mean) ** 2, axis=-1, keepdims=True)
    y = ((y - mean) * lax.rsqrt(var + RWKV_LN_EPS)).reshape(b, s, d)
    y = y * ln_w.astype(f32) + ln_b.astype(f32)
    bonus = jnp.sum(r4 * k4 * r_k.astype(f32), axis=-1, keepdims=True) * v4
    y = (y + bonus.reshape(b, s, d)) * g.astype(f32)
    return (y.astype(h.dtype) @ w_out).astype(h.dtype)


def _complex_affine_combine(e1, e2):
    ar1, ai1, br1, bi1 = e1
    ar2, ai2, br2, bi2 = e2
    return (ar1 * ar2 - ai1 * ai2,
            ar1 * ai2 + ai1 * ar2,
            ar2 * br1 - ai2 * bi1 + br2,
            ar2 * bi1 + ai2 * br1 + bi2)


def s5_mixer(h, w_in, lam_re, lam_im, log_step, b_re, b_im, c_re, c_im, d_skip, w_out):
    bsz, s, d = h.shape
    f32 = jnp.float32
    G, P, Q, L = S5_GROUPS, S5_STATE, S5_GROUP, S5_CHUNK
    u = (h @ w_in).astype(f32)
    lr = jnp.minimum(lam_re.astype(f32), S5_MAX_RE)
    li = lam_im.astype(f32)
    dt = jnp.exp(log_step.astype(f32))[:, None]
    mag = jnp.exp(lr * dt)
    abar_re, abar_im = mag * jnp.cos(li * dt), mag * jnp.sin(li * dt)
    den = lr * lr + li * li
    nr, ni = abar_re - 1.0, abar_im
    q_re, q_im = (nr * lr + ni * li) / den, (ni * lr - nr * li) / den
    br, bi = b_re.astype(f32), b_im.astype(f32)
    bbar_re = q_re[..., None] * br - q_im[..., None] * bi
    bbar_im = q_re[..., None] * bi + q_im[..., None] * br
    cr, ci = c_re.astype(f32), c_im.astype(f32)
    nc = s // L
    u_chunks = jnp.moveaxis(u.reshape(bsz, nc, L, G, Q), 1, 0)

    def chunk_step(carry, u_c):
        hr0, hi0 = carry
        bu_re = jnp.einsum('blgq,gpq->blgp', u_c, bbar_re)
        bu_im = jnp.einsum('blgq,gpq->blgp', u_c, bbar_im)
        bu_re = bu_re.at[:, 0].add(abar_re * hr0 - abar_im * hi0)
        bu_im = bu_im.at[:, 0].add(abar_re * hi0 + abar_im * hr0)
        ar = jnp.broadcast_to(abar_re, bu_re.shape)
        ai = jnp.broadcast_to(abar_im, bu_im.shape)
        _, _, hr, hi = lax.associative_scan(_complex_affine_combine, (ar, ai, bu_re, bu_im), axis=1)
        y = jnp.einsum('blgp,gqp->blgq', hr, cr) - jnp.einsum('blgp,gqp->blgq', hi, ci)
        return (hr[:, -1], hi[:, -1]), y

    carry0 = (jnp.zeros((bsz, G, P), f32), jnp.zeros((bsz, G, P), f32))
    _, y = lax.scan(chunk_step, carry0, u_chunks)
    y = jnp.moveaxis(y, 0, 1).reshape(bsz, s, d) + d_skip.astype(f32) * u
    y = jax.nn.gelu(y).astype(h.dtype)
    val, gate = jnp.split(y @ w_out, 2, axis=-1)
    return (val * jax.nn.sigmoid(gate)).astype(h.dtype)


def setup_inputs(seed: int = 0) -> dict:
    key = jax.random.key(seed)
    it = iter(jax.random.split(key, 48))
    nrm = lambda shape, scale: jax.random.normal(next(it), shape, jnp.float32) * scale
    D, F = D_MODEL, D_FF
    NA, NB, NC = N_FOX_LAYERS, N_RWKV_LAYERS, N_S5_LAYERS
    inp = {}
    inp['x'] = nrm((BATCH, SEQ, D), 1.0)
    inp['norm_w'] = 1.0 + nrm((DEPTH, 3, D), 0.02)
    inp['ffn_w_up'] = nrm((DEPTH, 2, D, 2 * F), D ** -0.5)
    inp['ffn_w_down'] = nrm((DEPTH, 2, F, D), F ** -0.5)
    inp['fox_w_in'] = nrm((NA, D, FOX_IN), D ** -0.5)
    inp['fox_b_f'] = 2.0 + nrm((NA, FOX_HEADS), 0.5)
    inp['fox_qk_gain'] = 1.0 + nrm((NA, 2, FOX_HEAD_DIM), 0.02)
    inp['fox_w_out'] = nrm((NA, D, D), D ** -0.5)
    inp['rwkv_mu'] = jax.random.uniform(next(it), (NB, 6, D), jnp.float32)
    inp['rwkv_w_rkv'] = nrm((NB, 3, D, D), D ** -0.5)
    ramp = jnp.linspace(0.0, 1.0, D, dtype=jnp.float32) ** 0.9
    inp['rwkv_w0'] = (-6.0 + 5.0 * ramp + 0.5)[None, :] + nrm((NB, D), 0.1)
    inp['rwkv_w1'] = nrm((NB, D, RWKV_DECAY_LORA), D ** -0.5)
    inp['rwkv_w2'] = nrm((NB, RWKV_DECAY_LORA, D), 0.1 * RWKV_DECAY_LORA ** -0.5)
    inp['rwkv_a0'] = nrm((NB, D), 0.1)
    inp['rwkv_a1'] = nrm((NB, D, RWKV_A_LORA), D ** -0.5)
    inp['rwkv_a2'] = nrm((NB, RWKV_A_LORA, D), 0.1 * RWKV_A_LORA ** -0.5)
    inp['rwkv_g1'] = nrm((NB, D, RWKV_GATE_LORA), D ** -0.5)
    inp['rwkv_g2'] = nrm((NB, RWKV_GATE_LORA, D), RWKV_GATE_LORA ** -0.5)
    inp['rwkv_k_k'] = 0.85 + nrm((NB, D), 0.02)
    inp['rwkv_k_a'] = 1.0 + nrm((NB, D), 0.02)
    inp['rwkv_r_k'] = nrm((NB, RWKV_HEADS, RWKV_HEAD_DIM), 0.1)
    inp['rwkv_ln_w'] = 1.0 + nrm((NB, D), 0.02)
    inp['rwkv_ln_b'] = nrm((NB, D), 0.02)
    inp['rwkv_w_out'] = nrm((NB, D, D), D ** -0.5)
    inp['s5_w_in'] = nrm((NC, D, D), D ** -0.5)
    inp['s5_lam_re'] = -0.5 + nrm((NC, S5_GROUPS, S5_STATE), 0.01)
    inp['s5_lam_im'] = (math.pi * jnp.arange(S5_STATE, dtype=jnp.float32))[None, None, :] + nrm((NC, S5_GROUPS, S5_STATE), 0.01)
    inp['s5_log_step'] = jax.random.uniform(next(it), (NC, S5_GROUPS), jnp.float32, math.log(S5_DT_MIN), math.log(S5_DT_MAX))
    inp['s5_b_re'] = nrm((NC, S5_GROUPS, S5_STATE, S5_GROUP), (2 * S5_GROUP) ** -0.5)
    inp['s5_b_im'] = nrm((NC, S5_GROUPS, S5_STATE, S5_GROUP), (2 * S5_GROUP) ** -0.5)
    inp['s5_c_re'] = nrm((NC, S5_GROUPS, S5_GROUP, S5_STATE), S5_STATE ** -0.5)
    inp['s5_c_im'] = nrm((NC, S5_GROUPS, S5_GROUP, S5_STATE), S5_STATE ** -0.5)
    inp['s5_d'] = nrm((NC, D), 1.0)
    inp['s5_w_out'] = nrm((NC, D, 2 * D), D ** -0.5)
    inp['final_norm'] = 1.0 + nrm((D,), 0.02)
    return inp


def reference(x, norm_w, ffn_w_up, ffn_w_down, fox_w_in, fox_b_f, fox_qk_gain, fox_w_out,
              rwkv_mu, rwkv_w_rkv, rwkv_w0, rwkv_w1, rwkv_w2, rwkv_a0, rwkv_a1, rwkv_a2,
              rwkv_g1, rwkv_g2, rwkv_k_k, rwkv_k_a, rwkv_r_k, rwkv_ln_w, rwkv_ln_b, rwkv_w_out,
              s5_w_in, s5_lam_re, s5_lam_im, s5_log_step, s5_b_re, s5_b_im, s5_c_re, s5_c_im,
              s5_d, s5_w_out, final_norm):
    ia = ib = ic = 0
    for i in range(DEPTH):
        x = x + 0.5 * swiglu(rms_norm(x, norm_w[i, 0]), ffn_w_up[i, 0], ffn_w_down[i, 0])
        h = rms_norm(x, norm_w[i, 1])
        m = i % N_MIXERS
        if m == 0:
            x = x + fox_mixer(h, fox_w_in[ia], fox_b_f[ia], fox_qk_gain[ia], fox_w_out[ia])
            ia += 1
        elif m == 1:
            x = x + rwkv7_mixer(h, rwkv_mu[ib], rwkv_w_rkv[ib], rwkv_w0[ib], rwkv_w1[ib], rwkv_w2[ib],
                                rwkv_a0[ib], rwkv_a1[ib], rwkv_a2[ib], rwkv_g1[ib], rwkv_g2[ib],
                                rwkv_k_k[ib], rwkv_k_a[ib], rwkv_r_k[ib], rwkv_ln_w[ib], rwkv_ln_b[ib],
                                rwkv_w_out[ib])
            ib += 1
        else:
            x = x + s5_mixer(h, s5_w_in[ic], s5_lam_re[ic], s5_lam_im[ic], s5_log_step[ic],
                             s5_b_re[ic], s5_b_im[ic], s5_c_re[ic], s5_c_im[ic], s5_d[ic], s5_w_out[ic])
            ic += 1
        x = x + 0.5 * swiglu(rms_norm(x, norm_w[i, 2]), ffn_w_up[i, 1], ffn_w_down[i, 1])
    return rms_norm(x, final_norm)
```

```python
import functools
import math

import jax
import jax.numpy as jnp
from jax import lax
from jax.experimental import pallas as pl
from jax.experimental.pallas import tpu as pltpu

F32 = jnp.float32
BF16 = jnp.bfloat16

V7X_VMEM_BYTES = 64 * 1024 * 1024
VMEM_LIMIT_BYTES = V7X_VMEM_BYTES - 8 * 1024 * 1024
LANES = 128

RMS_EPS = 1e-6
FOX_HEAD_DIM = 128
RWKV_HEAD_DIM = 64
RWKV_LN_EPS = 64e-5
RWKV_NORM_EPS = 1e-12
RWKV_CHUNK = 64
S5_GROUP = 16
S5_STATE = 64
S5_MAX_RE = -1e-4
S5_GROUPS_PER_BLOCK = LANES // S5_GROUP
NEG_BIG = -1e30
S5_SCAN_ROWS = 8

NT_DIMS = (((1,), (1,)), ((), ()))
TN_DIMS = (((0,), (0,)), ((), ()))
NN_DIMS = (((1,), (0,)), ((), ()))


def _cparams(*sem):
    return pltpu.CompilerParams(dimension_semantics=sem, vmem_limit_bytes=VMEM_LIMIT_BYTES)


def _rms_scale(x):
    return lax.rsqrt(jnp.mean(x * x, axis=-1, keepdims=True) + RMS_EPS)


def _softplus(z):
    return jnp.maximum(z, 0.0) + jnp.log1p(jnp.exp(-jnp.abs(z)))


def _dot(a, b, dims=NN_DIMS):
    return lax.dot_general(a, b, dims, preferred_element_type=F32)


def _split_bf16(x, parts):
    out = []
    rem = x
    for _ in range(parts):
        p = rem.astype(BF16)
        out.append(p)
        rem = rem - p.astype(F32)
    return out


def _dot_f32(a, b, dims=NN_DIMS, passes=3):
    if passes == 1:
        return _dot(a.astype(BF16), b.astype(BF16), dims)
    a_p = _split_bf16(a, 2 if passes == 3 else 3)
    b_p = _split_bf16(b, 2 if passes == 3 else 3)
    acc = None
    for ia, ap in enumerate(a_p):
        for ib, bp in enumerate(b_p):
            if ia + ib >= len(a_p):
                continue
            t = _dot(ap, bp, dims)
            acc = t if acc is None else acc + t
    return acc


def _ffn_kernel(x_ref, g_ref, wg_ref, wu_ref, wd_ref, *rest, final_norm):
    if final_norm:
        fin_ref, o_ref, h_ref = rest
    else:
        o_ref, h_ref = rest
    j = pl.program_id(1)

    @pl.when(j == 0)
    def _():
        x = x_ref[...]
        h_ref[...] = (x * _rms_scale(x) * g_ref[...]).astype(BF16)
        o_ref[...] = jnp.zeros_like(o_ref)

    h = h_ref[...]
    gate = _dot(h, wg_ref[...])
    up = _dot(h, wu_ref[...])
    act = (gate * jax.nn.sigmoid(gate) * up).astype(BF16)
    o_ref[...] += _dot(act, wd_ref[...])

    @pl.when(j == pl.num_programs(1) - 1)
    def _():
        y = x_ref[...] + 0.5 * o_ref[...]
        if final_norm:
            y = y * _rms_scale(y) * fin_ref[...]
        o_ref[...] = y


def _ffn(x, g, w_up, w_down, fin=None, *, tm=512, tf=512):
    m, d = x.shape
    f = w_down.shape[0]
    tm = min(tm, m)
    nf = f // tf
    in_specs = [
        pl.BlockSpec((tm, d), lambda i, j: (i, 0)),
        pl.BlockSpec((1, d), lambda i, j: (0, 0)),
        pl.BlockSpec((d, tf), lambda i, j: (0, j)),
        pl.BlockSpec((d, tf), lambda i, j: (0, j + nf)),
        pl.BlockSpec((tf, d), lambda i, j: (j, 0)),
    ]
    args = [x, g.reshape(1, d), w_up, w_up, w_down]
    if fin is not None:
        in_specs.append(pl.BlockSpec((1, d), lambda i, j: (0, 0)))
        args.append(fin.reshape(1, d))
    return pl.pallas_call(
        functools.partial(_ffn_kernel, final_norm=fin is not None),
        grid=(m // tm, nf),
        in_specs=in_specs,
        out_specs=pl.BlockSpec((tm, d), lambda i, j: (i, 0)),
        out_shape=jax.ShapeDtypeStruct((m, d), F32),
        scratch_shapes=[pltpu.VMEM((tm, d), BF16)],
        compiler_params=_cparams("parallel", "arbitrary"),
        name="ffn",
    )(*args)


def _mm_kernel(*refs, n_b, n_row, n_tile, n_out, rmsnorm, epilogue):
    pos = 0
    a_ref = refs[pos]; pos += 1
    if rmsnorm:
        g_ref = refs[pos]; pos += 1
    b_refs = refs[pos:pos + n_b]; pos += n_b
    row_refs = refs[pos:pos + n_row]; pos += n_row
    tile_refs = refs[pos:pos + n_tile]; pos += n_tile
    out_refs = refs[pos:pos + n_out]; pos += n_out
    if rmsnorm:
        h_ref = refs[pos]

        @pl.when(pl.program_id(1) == 0)
        def _():
            x = a_ref[...]
            h_ref[...] = (x * _rms_scale(x) * g_ref[...]).astype(BF16)

        a = h_ref[...]
    else:
        a = a_ref[...]
    accs = [_dot(a, b_ref[...]) for b_ref in b_refs]
    outs = epilogue(accs, [r[...] for r in row_refs], [t[...] for t in tile_refs])
    for o_ref, o in zip(out_refs, outs):
        o_ref[...] = o.astype(o_ref.dtype)


def _mm(a, bs, epilogue, out_dtypes, *, n, rows=(), tiles=(), gain=None, tm=512, tn=512):
    m, k = a.shape
    tm = min(tm, m)
    tn = min(tn, n)
    rmsnorm = gain is not None
    in_specs = [pl.BlockSpec((tm, k), lambda i, j: (i, 0))]
    args = [a]
    if rmsnorm:
        in_specs.append(pl.BlockSpec((1, k), lambda i, j: (0, 0)))
        args.append(gain.reshape(1, k))
    for b, off in bs:
        ob = off // tn
        in_specs.append(pl.BlockSpec((k, tn), lambda i, j, ob=ob: (0, j + ob)))
        args.append(b)
    for r in rows:
        if r.shape[1] == tn and n != tn:
            in_specs.append(pl.BlockSpec((1, tn), lambda i, j: (0, 0)))
        else:
            in_specs.append(pl.BlockSpec((1, tn), lambda i, j: (0, j)))
        args.append(r)
    for t in tiles:
        in_specs.append(pl.BlockSpec((tm, tn), lambda i, j: (i, j)))
        args.append(t)
    out_shape = [jax.ShapeDtypeStruct((m, n), dt) for dt in out_dtypes]
    out_specs = [pl.BlockSpec((tm, tn), lambda i, j: (i, j)) for _ in out_dtypes]
    kern = functools.partial(
        _mm_kernel, n_b=len(bs), n_row=len(rows), n_tile=len(tiles), n_out=len(out_dtypes),
        rmsnorm=rmsnorm, epilogue=epilogue)
    return pl.pallas_call(
        kern,
        grid=(m // tm, n // tn),
        in_specs=in_specs,
        out_specs=out_specs,
        out_shape=out_shape,
        scratch_shapes=[pltpu.VMEM((tm, k), BF16)] if rmsnorm else [],
        compiler_params=_cparams("parallel", "arbitrary"),
        name="proj",
    )(*args)


def _fox_qkvg_epilogue(accs, rows, tiles):
    gq, gk = rows
    q, k, v, g = accs
    tn = q.shape[1]

    def head_norm(t, gain):
        parts = []
        for hh in range(tn // FOX_HEAD_DIM):
            sl = t[:, hh * FOX_HEAD_DIM:(hh + 1) * FOX_HEAD_DIM]
            parts.append(sl * _rms_scale(sl))
        return jnp.concatenate(parts, axis=1) * gain

    return head_norm(q, gq), head_norm(k, gk), v, jax.nn.sigmoid(g)


def _fox_forget_kernel(x_ref, g_ref, wf_ref, bf_ref, tri_ref, c_ref, carry_ref):
    i = pl.program_id(0)

    @pl.when(i == 0)
    def _():
        carry_ref[...] = jnp.zeros_like(carry_ref)

    x = x_ref[...]
    h = (x * _rms_scale(x) * g_ref[...]).astype(BF16)
    f_logit = _dot(wf_ref[...], h, NT_DIMS) + bf_ref[...]
    log_f = -_softplus(-f_logit)
    tri = tri_ref[...]
    local = None
    for part in _split_bf16(log_f, 3):
        t = _dot(part, tri)
        local = t if local is None else local + t
    c = local + carry_ref[...]
    c_ref[...] = c
    carry_ref[...] = c[:, -1:]


def _fox_forget(x, g, wf_t, b_f, *, tm=512):
    m, d = x.shape
    nh = wf_t.shape[0]
    tm = min(tm, m)
    tri = jnp.triu(jnp.ones((tm, tm), F32)).astype(BF16)
    return pl.pallas_call(
        _fox_forget_kernel,
        grid=(m // tm,),
        in_specs=[
            pl.BlockSpec((tm, d), lambda i: (i, 0)),
            pl.BlockSpec((1, d), lambda i: (0, 0)),
            pl.BlockSpec((nh, d), lambda i: (0, 0)),
            pl.BlockSpec((nh, 1), lambda i: (0, 0)),
            pl.BlockSpec((tm, tm), lambda i: (0, 0)),
        ],
        out_specs=pl.BlockSpec((nh, tm), lambda i: (0, i)),
        out_shape=jax.ShapeDtypeStruct((nh, m), F32),
        scratch_shapes=[pltpu.VMEM((nh, 1), F32)],
        compiler_params=_cparams("arbitrary"),
        name="fox_forget",
    )(x, g.reshape(1, d), wf_t, b_f.reshape(nh, 1), tri)


def _fox_attn_kernel(qi_ref, kj_ref, last_ref, q_ref, k_ref, v_ref, cq_ref, ck_ref, g_ref,
                     o_ref, m_ref, l_ref, acc_ref, *, tq, tk):
    p = pl.program_id(1)
    i = qi_ref[p]
    j = kj_ref[p]

    @pl.when(j == 0)
    def _():
        m_ref[...] = jnp.full_like(m_ref, NEG_BIG)
        l_ref[...] = jnp.zeros_like(l_ref)
        acc_ref[...] = jnp.zeros_like(acc_ref)

    s = _dot(q_ref[...], k_ref[...], NT_DIMS)
    c_first = cq_ref[0][:, 0:1]
    s = s + (c_first - ck_ref[0])
    row = i * tq + lax.broadcasted_iota(jnp.int32, (tq, tk), 0)
    col = j * tk + lax.broadcasted_iota(jnp.int32, (tq, tk), 1)
    s = jnp.where(col <= row, s, NEG_BIG)
    m_prev = m_ref[...]
    m_new = jnp.maximum(m_prev, jnp.max(s, axis=-1, keepdims=True))
    alpha = jnp.exp(m_prev - m_new)
    pmat = jnp.exp(s - m_new)
    l_ref[...] = alpha * l_ref[...] + jnp.sum(pmat, axis=-1, keepdims=True)
    acc_ref[...] = alpha * acc_ref[...] + _dot(pmat.astype(BF16), v_ref[...])
    m_ref[...] = m_new

    @pl.when(last_ref[p] == 1)
    def _():
        o = acc_ref[...] / l_ref[...]
        o_ref[...] = (o * g_ref[...].astype(F32)).astype(o_ref.dtype)


def _fox_attention(q, k, v, c, gate, *, tq=512, tk=512):
    m, d = q.shape
    nh = d // FOX_HEAD_DIM
    tq = min(tq, m)
    tk = min(tk, m)
    qi, kj, last = [], [], []
    for i in range(m // tq):
        j_last = ((i + 1) * tq - 1) // tk
        for j in range(j_last + 1):
            qi.append(i); kj.append(j); last.append(1 if j == j_last else 0)
    n_pairs = len(qi)
    qi = jnp.asarray(qi, jnp.int32)
    kj = jnp.asarray(kj, jnp.int32)
    last = jnp.asarray(last, jnp.int32)
    c3 = c.reshape(nh, 1, m)
    dh = FOX_HEAD_DIM
    grid_spec = pltpu.PrefetchScalarGridSpec(
        num_scalar_prefetch=3,
        grid=(nh, n_pairs),
        in_specs=[
            pl.BlockSpec((tq, dh), lambda h, p, qi, kj, la: (qi[p], h)),
            pl.BlockSpec((tk, dh), lambda h, p, qi, kj, la: (kj[p], h)),
            pl.BlockSpec((tk, dh), lambda h, p, qi, kj, la: (kj[p], h)),
            pl.BlockSpec((1, 1, tq), lambda h, p, qi, kj, la: (h, 0, qi[p])),
            pl.BlockSpec((1, 1, tk), lambda h, p, qi, kj, la: (h, 0, kj[p])),
            pl.BlockSpec((tq, dh), lambda h, p, qi, kj, la: (qi[p], h)),
        ],
        out_specs=pl.BlockSpec((tq, dh), lambda h, p, qi, kj, la: (qi[p], h)),
        scratch_shapes=[
            pltpu.VMEM((tq, 1), F32),
            pltpu.VMEM((tq, 1), F32),
            pltpu.VMEM((tq, dh), F32),
        ],
    )
    return pl.pallas_call(
        functools.partial(_fox_attn_kernel, tq=tq, tk=tk),
        grid_spec=grid_spec,
        out_shape=jax.ShapeDtypeStruct((m, d), BF16),
        compiler_params=_cparams("parallel", "arbitrary"),
        name="fox_attn",
    )(qi, kj, last, q, k, v, c3, c3, gate)


def _fox_layer(x, norm_g, w_in, b_f, qk_gain, w_out):
    m, d = x.shape
    nh = d // FOX_HEAD_DIM
    w_in_b = w_in.astype(BF16)
    tn = 512
    scale = FOX_HEAD_DIM ** -0.5
    gq = jnp.tile(qk_gain[0] * scale, tn // FOX_HEAD_DIM).reshape(1, tn)
    gk = jnp.tile(qk_gain[1], tn // FOX_HEAD_DIM).reshape(1, tn)
    q, k, v, gate = _mm(
        x, [(w_in_b, 0), (w_in_b, d), (w_in_b, 2 * d), (w_in_b, 3 * d)],
        _fox_qkvg_epilogue, [BF16, BF16, BF16, BF16], n=d, rows=[gq, gk], gain=norm_g, tn=tn)
    wf_t = w_in[:, 4 * d:].T.astype(BF16)
    c = _fox_forget(x, norm_g, wf_t, b_f)
    og = _fox_attention(q, k, v, c, gate)
    (x_new,) = _mm(og, [(w_out.astype(BF16), 0)],
                   lambda accs, rows, tiles: (tiles[0] + accs[0],), [F32], n=d, tiles=[x])
    return x_new


def _rwkv_prep_kernel(x_ref, xp_ref, g_ref, mu_ref, *rest):
    out_refs = rest[:6]
    hbuf = rest[6]
    i = pl.program_id(0)
    tm = x_ref.shape[0]
    x = x_ref[...]
    g = g_ref[...]
    h = x * _rms_scale(x) * g
    xp = xp_ref[...][7:8, :]
    hp = xp * _rms_scale(xp) * g
    hp = jnp.where(i == 0, jnp.zeros_like(hp), hp)
    hbuf[pl.ds(8, tm), :] = h
    hbuf[pl.ds(7, 1), :] = hp
    xx = hbuf[pl.ds(7, tm), :] - h
    mu = mu_ref[...]
    for n, o_ref in enumerate(out_refs):
        o_ref[...] = (h + xx * mu[n:n + 1, :]).astype(o_ref.dtype)


def _rwkv_prep(x, g, mu, *, tm=256):
    m, d = x.shape
    tm = min(tm, m)
    rb = tm // 8
    return pl.pallas_call(
        _rwkv_prep_kernel,
        grid=(m // tm,),
        in_specs=[
            pl.BlockSpec((tm, d), lambda i: (i, 0)),
            pl.BlockSpec((8, d), lambda i: (jnp.maximum(i * rb - 1, 0), 0)),
            pl.BlockSpec((1, d), lambda i: (0, 0)),
            pl.BlockSpec((8, d), lambda i: (0, 0)),
        ],
        out_specs=[pl.BlockSpec((tm, d), lambda i: (i, 0)) for _ in range(6)],
        out_shape=[jax.ShapeDtypeStruct((m, d), BF16) for _ in range(6)],
        scratch_shapes=[pltpu.VMEM((tm + 8, d), F32)],
        compiler_params=_cparams("parallel"),
        name="rwkv_prep",
    )(x, x, g.reshape(1, d), jnp.pad(mu, ((0, 2), (0, 0))))


def _lora_kernel(x_ref, w1_ref, w2_ref, b_ref, o_ref, *, mid_act, out_act):
    t = _dot(x_ref[...], w1_ref[...])
    t = mid_act(t).astype(BF16)
    y = _dot(t, w2_ref[...]) + b_ref[...]
    o_ref[...] = out_act(y).astype(o_ref.dtype)


def _lora(x, w1, w2, bias, mid_act, out_act, *, tm=512):
    m, d = x.shape
    r = w1.shape[1]
    n = w2.shape[1]
    tm = min(tm, m)
    return pl.pallas_call(
        functools.partial(_lora_kernel, mid_act=mid_act, out_act=out_act),
        grid=(m // tm,),
        in_specs=[
            pl.BlockSpec((tm, d), lambda i: (i, 0)),
            pl.BlockSpec((d, r), lambda i: (0, 0)),
            pl.BlockSpec((r, n), lambda i: (0, 0)),
            pl.BlockSpec((1, n), lambda i: (0, 0)),
        ],
        out_specs=pl.BlockSpec((tm, n), lambda i: (i, 0)),
        out_shape=jax.ShapeDtypeStruct((m, n), F32),
        compiler_params=_cparams("parallel"),
        name="rwkv_lora",
    )(x, w1.astype(BF16), w2.astype(BF16), bias.reshape(1, n))


def _tri_mask(n, strict):
    row = lax.broadcasted_iota(jnp.int32, (n, n), 0)
    col = lax.broadcasted_iota(jnp.int32, (n, n), 1)
    return (col < row) if strict else (col <= row)


def _rwkv_chunk(r, k, v, ld, a, st, k_k, k_a, tri_incl):
    c, n = r.shape
    kk = k * k_k
    kk = kk / jnp.maximum(jnp.sqrt(jnp.sum(kk * kk, axis=-1, keepdims=True)), RWKV_NORM_EPS)
    k2 = k * (1.0 + (a - 1.0) * k_a)
    av = -kk
    bv = kk * a
    gam = None
    for part in _split_bf16(ld, 3):
        t = _dot(tri_incl, part)
        gam = t if gam is None else gam + t
    g_end = gam[c - 1:c, :]
    at = av * jnp.exp(gam - ld)
    rt = r * jnp.exp(gam)
    e_neg = jnp.exp(-gam)
    bt = bv * e_neg
    kt = k2 * e_neg
    e_end = jnp.exp(g_end - gam)
    b_end = bv * e_end
    k_end = k2 * e_end
    strict = _tri_mask(c, True)
    incl = _tri_mask(c, False)
    g_ab = jnp.where(strict, _dot_f32(at, bt, NT_DIMS), 0.0)
    g_ak = jnp.where(strict, _dot_f32(at, kt, NT_DIMS), 0.0)
    g_rb = jnp.where(incl, _dot_f32(rt, bt, NT_DIMS), 0.0)
    g_rk = jnp.where(incl, _dot_f32(rt, kt, NT_DIMS), 0.0)
    eye = (lax.broadcasted_iota(jnp.int32, (c, c), 0) == lax.broadcasted_iota(jnp.int32, (c, c), 1)).astype(F32)
    tinv = eye + g_ab
    npow = g_ab
    for _ in range(int(math.log2(c)) - 1):
        npow = _dot_f32(npow, npow)
        tinv = tinv + _dot_f32(tinv, npow)
    av_v = _dot_f32(g_ak, v)
    pmat = _dot_f32(tinv, at)
    qmat = _dot_f32(tinv, av_v)
    eye_n = (lax.broadcasted_iota(jnp.int32, (n, n), 0) == lax.broadcasted_iota(jnp.int32, (n, n), 1)).astype(F32)
    m_mat = _dot_f32(b_end, pmat, TN_DIMS) + eye_n * jnp.exp(g_end)
    n_mat = _dot_f32(b_end, qmat, TN_DIMS) + _dot_f32(k_end, v, TN_DIMS)
    o1 = rt + _dot_f32(g_rb, pmat)
    o2 = _dot_f32(g_rb, qmat) + _dot_f32(g_rk, v)
    o = _dot_f32(o1, st) + o2
    st_new = _dot_f32(m_mat, st) + n_mat
    return o, st_new, k2


def _rwkv_core_kernel(r_ref, k_ref, v_ref, ld_ref, a_ref, g_ref, kk_ref, ka_ref, rk_ref,
                      lnw_ref, lnb_ref, tri_ref, o_ref, st_ref, *, chunk):
    t_idx = pl.program_id(1)

    @pl.when(t_idx == 0)
    def _():
        st_ref[...] = jnp.zeros_like(st_ref)

    tt = r_ref.shape[0]
    n = RWKV_HEAD_DIM
    tri = tri_ref[...]
    for hh in range(LANES // n):
        ls = slice(hh * n, (hh + 1) * n)
        k_k = kk_ref[:, ls]
        k_a = ka_ref[:, ls]
        r_k = rk_ref[:, ls]
        ln_w = lnw_ref[:, ls]
        ln_b = lnb_ref[:, ls]
        st = st_ref[hh]
        for cc in range(tt // chunk):
            rs = slice(cc * chunk, (cc + 1) * chunk)
            r = r_ref[rs, ls]
            k = k_ref[rs, ls]
            v = v_ref[rs, ls]
            o, st, k2 = _rwkv_chunk(r, k, v, ld_ref[rs, ls], a_ref[rs, ls], st, k_k, k_a, tri)
            mean = jnp.mean(o, axis=-1, keepdims=True)
            cen = o - mean
            var = jnp.mean(cen * cen, axis=-1, keepdims=True)
            y = cen * lax.rsqrt(var + RWKV_LN_EPS) * ln_w + ln_b
            bonus = jnp.sum(r * k2 * r_k, axis=-1, keepdims=True) * v
            o_ref[rs, ls] = ((y + bonus) * g_ref[rs, ls]).astype(o_ref.dtype)
        st_ref[hh] = st


def _rwkv_core(r, k, v, ld, a, g, k_k, k_a, r_k, ln_w, ln_b, *, tt=256):
    m, d = r.shape
    tt = min(tt, m)
    chunk = min(RWKV_CHUNK, tt)
    tri = jnp.tril(jnp.ones((chunk, chunk), F32)).astype(BF16)
    seq = pl.BlockSpec((tt, LANES), lambda hp, t: (t, hp))
    par = pl.BlockSpec((1, LANES), lambda hp, t: (0, hp))
    return pl.pallas_call(
        functools.partial(_rwkv_core_kernel, chunk=chunk),
        grid=(d // LANES, m // tt),
        in_specs=[seq] * 6 + [par] * 5 + [pl.BlockSpec((chunk, chunk), lambda hp, t: (0, 0))],
        out_specs=seq,
        out_shape=jax.ShapeDtypeStruct((m, d), BF16),
        scratch_shapes=[pltpu.VMEM((LANES // RWKV_HEAD_DIM, RWKV_HEAD_DIM, RWKV_HEAD_DIM), F32)],
        compiler_params=_cparams("parallel", "arbitrary"),
        name="rwkv_core",
    )(r, k, v, ld, a, g, k_k.reshape(1, d), k_a.reshape(1, d), r_k.reshape(1, d),
      ln_w.reshape(1, d), ln_b.reshape(1, d), tri)


def _rwkv_layer(x, norm_g, mu, w_rkv, w0, w1, w2, a0, a1, a2, g1, g2, k_k, k_a, r_k, ln_w, ln_b, w_out):
    m, d = x.shape
    xr, xw, xk, xv, xa, xg = _rwkv_prep(x, norm_g, mu)
    w_rkv_b = w_rkv.astype(BF16)
    plain = lambda accs, rows, tiles: (accs[0],)
    (r,) = _mm(xr, [(w_rkv_b[0], 0)], plain, [F32], n=d)
    (k,) = _mm(xk, [(w_rkv_b[1], 0)], plain, [F32], n=d)
    (v,) = _mm(xv, [(w_rkv_b[2], 0)], plain, [F32], n=d)
    ident = lambda t: t
    ld = _lora(xw, w1, w2, w0, jnp.tanh, lambda y: -jnp.exp(-_softplus(-y) - 0.5))
    a = _lora(xa, a1, a2, a0, ident, jax.nn.sigmoid)
    g = _lora(xg, g1, g2, jnp.zeros((d,), F32), jax.nn.sigmoid, ident)
    y = _rwkv_core(r, k, v, ld, a, g, k_k, k_a, r_k, ln_w, ln_b)
    (x_new,) = _mm(y, [(w_out.astype(BF16), 0)],
                   lambda accs, rows, tiles: (tiles[0] + accs[0],), [F32], n=d, tiles=[x])
    return x_new


def _s5_core_kernel(u_ref, b_ref, a_ref, c_ref, d_ref, o_ref, hbuf, carry_ref, *, pad):
    t_idx = pl.program_id(1)
    tl = u_ref.shape[0]
    half = a_ref.shape[2] // 2

    @pl.when(t_idx == 0)
    def _():
        carry_ref[...] = jnp.zeros_like(carry_ref)
        hbuf[pl.ds(0, pad), :] = jnp.zeros((pad, hbuf.shape[1]), F32)

    u = u_ref[...]
    bu = _dot(u.astype(BF16), b_ref[0])
    a_row = a_ref[0]
    ar = a_row[:, :half]
    ai = a_row[:, half:]
    hr0 = carry_ref[0:1, :half]
    hi0 = carry_ref[0:1, half:]
    hbuf[pl.ds(pad, tl), :] = bu
    first = hbuf[pl.ds(pad, 1), :]
    hbuf[pl.ds(pad, 1), :] = first + jnp.concatenate(
        [ar * hr0 - ai * hi0, ar * hi0 + ai * hr0], axis=1)
    rc = S5_SCAN_ROWS
    s = 1
    while s < tl:
        def level(n, carry, s=s, ar=ar, ai=ai):
            t0 = pl.multiple_of(pad + tl - (n + 1) * rc, rc)
            xr = hbuf[pl.ds(t0, rc), pl.ds(0, half)]
            xi = hbuf[pl.ds(t0, rc), pl.ds(half, half)]
            if s >= 8:
                ts = pl.multiple_of(t0 - s, 8)
                sr = hbuf[pl.ds(ts, rc), pl.ds(0, half)]
                si = hbuf[pl.ds(ts, rc), pl.ds(half, half)]
            else:
                tp = pl.multiple_of(t0 - 8, 8)
                sr = pltpu.roll(jnp.concatenate([hbuf[pl.ds(tp, 8), pl.ds(0, half)], xr], axis=0), s, axis=0)[8:, :]
                si = pltpu.roll(jnp.concatenate([hbuf[pl.ds(tp, 8), pl.ds(half, half)], xi], axis=0), s, axis=0)[8:, :]
            hbuf[pl.ds(t0, rc), pl.ds(0, half)] = xr + ar * sr - ai * si
            hbuf[pl.ds(t0, rc), pl.ds(half, half)] = xi + ar * si + ai * sr
            return carry
        lax.fori_loop(0, tl // rc, level, 0)
        ar, ai = ar * ar - ai * ai, 2.0 * ar * ai
        s *= 2
    hfin = hbuf[pl.ds(pad, tl), :]
    carry_ref[0:1, :] = hfin[tl - 1:tl, :]
    y = _dot(hfin.astype(BF16), c_ref[0]) + d_ref[...] * u
    o_ref[...] = jax.nn.gelu(y).astype(o_ref.dtype)


def _s5_core(u, b_blk, a_blk, c_blk, d_skip, *, tl=256):
    m, d = u.shape
    tl = min(tl, m)
    pad = tl // 2 if tl >= 16 else 8
    nb = d // LANES
    width = b_blk.shape[2]
    return pl.pallas_call(
        functools.partial(_s5_core_kernel, pad=pad),
        grid=(nb, m // tl),
        in_specs=[
            pl.BlockSpec((tl, LANES), lambda gb, t: (t, gb)),
            pl.BlockSpec((1, LANES, width), lambda gb, t: (gb, 0, 0)),
            pl.BlockSpec((1, 1, width), lambda gb, t: (gb, 0, 0)),
            pl.BlockSpec((1, width, LANES), lambda gb, t: (gb, 0, 0)),
            pl.BlockSpec((1, LANES), lambda gb, t: (0, gb)),
        ],
        out_specs=pl.BlockSpec((tl, LANES), lambda gb, t: (t, gb)),
        out_shape=jax.ShapeDtypeStruct((m, d), BF16),
        scratch_shapes=[pltpu.VMEM((pad + tl, width), F32), pltpu.VMEM((8, width), F32)],
        compiler_params=_cparams("parallel", "arbitrary"),
        name="s5_core",
    )(u, b_blk, a_blk, c_blk, d_skip.reshape(1, d))


def _s5_tables(lam_re, lam_im, log_step, b_re, b_im, c_re, c_im):
    g, p = lam_re.shape
    q = b_re.shape[2]
    gpb = S5_GROUPS_PER_BLOCK
    nb = g // gpb
    lr = jnp.minimum(lam_re.astype(F32), S5_MAX_RE)
    li = lam_im.astype(F32)
    dt = jnp.exp(log_step.astype(F32))[:, None]
    mag = jnp.exp(lr * dt)
    abar_re, abar_im = mag * jnp.cos(li * dt), mag * jnp.sin(li * dt)
    den = lr * lr + li * li
    nr, ni = abar_re - 1.0, abar_im
    q_re, q_im = (nr * lr + ni * li) / den, (ni * lr - nr * li) / den
    br, bi = b_re.astype(F32), b_im.astype(F32)
    bbar_re = q_re[..., None] * br - q_im[..., None] * bi
    bbar_im = q_re[..., None] * bi + q_im[..., None] * br
    eye = jnp.eye(gpb, dtype=F32)

    def blockdiag_in(bb):
        t = bb.reshape(nb, gpb, p, q).transpose(0, 1, 3, 2)
        return jnp.einsum('ngqp,gh->ngqhp', t, eye).reshape(nb, gpb * q, gpb * p)

    def blockdiag_out(cc):
        t = cc.reshape(nb, gpb, q, p).transpose(0, 1, 3, 2)
        return jnp.einsum('ngpq,gh->ngphq', t, eye).reshape(nb, gpb * p, gpb * q)

    b_blk = jnp.concatenate([blockdiag_in(bbar_re), blockdiag_in(bbar_im)], axis=2).astype(BF16)
    c_blk = jnp.concatenate([blockdiag_out(c_re.astype(F32)), -blockdiag_out(c_im.astype(F32))],
                            axis=1).astype(BF16)
    a_blk = jnp.concatenate([abar_re.reshape(nb, 1, gpb * p), abar_im.reshape(nb, 1, gpb * p)], axis=2)
    return b_blk, a_blk, c_blk


def _s5_layer(x, norm_g, w_in, lam_re, lam_im, log_step, b_re, b_im, c_re, c_im, d_skip, w_out):
    m, d = x.shape
    (u,) = _mm(x, [(w_in.astype(BF16), 0)], lambda accs, rows, tiles: (accs[0],), [F32],
               n=d, gain=norm_g)
    b_blk, a_blk, c_blk = _s5_tables(lam_re, lam_im, log_step, b_re, b_im, c_re, c_im)
    y = _s5_core(u, b_blk, a_blk, c_blk, d_skip)
    w_out_b = w_out.astype(BF16)
    (x_new,) = _mm(y, [(w_out_b, 0), (w_out_b, d)],
                   lambda accs, rows, tiles: (tiles[0] + accs[0] * jax.nn.sigmoid(accs[1]),),
                   [F32], n=d, tiles=[x])
    return x_new


def kernel(x, norm_w, ffn_w_up, ffn_w_down, fox_w_in, fox_b_f, fox_qk_gain, fox_w_out, rwkv_mu, rwkv_w_rkv, rwkv_w0, rwkv_w1, rwkv_w2, rwkv_a0, rwkv_a1, rwkv_a2, rwkv_g1, rwkv_g2, rwkv_k_k, rwkv_k_a, rwkv_r_k, rwkv_ln_w, rwkv_ln_b, rwkv_w_out, s5_w_in, s5_lam_re, s5_lam_im, s5_log_step, s5_b_re, s5_b_im, s5_c_re, s5_c_im, s5_d, s5_w_out, final_norm):
    bsz, s, d = x.shape
    depth = norm_w.shape[0]
    outs = []
    for b in range(bsz):
        xb = x[b]
        ia = ib = ic = 0
        for i in range(depth):
            xb = _ffn(xb, norm_w[i, 0], ffn_w_up[i, 0].astype(BF16), ffn_w_down[i, 0].astype(BF16))
            mixer = i % 3
            if mixer == 0:
                xb = _fox_layer(xb, norm_w[i, 1], fox_w_in[ia], fox_b_f[ia], fox_qk_gain[ia], fox_w_out[ia])
                ia += 1
            elif mixer == 1:
                xb = _rwkv_layer(xb, norm_w[i, 1], rwkv_mu[ib], rwkv_w_rkv[ib], rwkv_w0[ib], rwkv_w1[ib],
                                 rwkv_w2[ib], rwkv_a0[ib], rwkv_a1[ib], rwkv_a2[ib], rwkv_g1[ib], rwkv_g2[ib],
                                 rwkv_k_k[ib], rwkv_k_a[ib], rwkv_r_k[ib].reshape(-1), rwkv_ln_w[ib],
                                 rwkv_ln_b[ib], rwkv_w_out[ib])
                ib += 1
            else:
                xb = _s5_layer(xb, norm_w[i, 1], s5_w_in[ic], s5_lam_re[ic], s5_lam_im[ic], s5_log_step[ic],
                               s5_b_re[ic], s5_b_im[ic], s5_c_re[ic], s5_c_im[ic], s5_d[ic], s5_w_out[ic])
                ic += 1
            fin = final_norm if i == depth - 1 else None
            xb = _ffn(xb, norm_w[i, 2], ffn_w_up[i, 1].astype(BF16), ffn_w_down[i, 1].astype(BF16), fin)
        outs.append(xb)
    return jnp.stack(outs, axis=0)
```

```python
import functools
import math

import jax
import jax.numpy as jnp
from jax import lax
from jax.experimental import pallas as pl
from jax.experimental.pallas import tpu as pltpu

F32 = jnp.float32
BF16 = jnp.bfloat16

V7X_VMEM_BYTES = 64 * 1024 * 1024
VMEM_LIMIT_BYTES = V7X_VMEM_BYTES - 8 * 1024 * 1024
LANES = 128

RMS_EPS = 1e-6
FOX_HEAD_DIM = 128
RWKV_HEAD_DIM = 64
RWKV_LN_EPS = 64e-5
RWKV_NORM_EPS = 1e-12
RWKV_CHUNK = 64
S5_GROUP = 16
S5_STATE = 64
S5_MAX_RE = -1e-4
S5_GROUPS_PER_BLOCK = LANES // S5_GROUP
NEG_BIG = -1e30
LOG2_E = math.log2(math.e)
S5_SCAN_ROWS = 8

NT_DIMS = (((1,), (1,)), ((), ()))
TN_DIMS = (((0,), (0,)), ((), ()))
NN_DIMS = (((1,), (0,)), ((), ()))


def _cparams(*sem):
    return pltpu.CompilerParams(dimension_semantics=sem, vmem_limit_bytes=VMEM_LIMIT_BYTES)


def _rms_scale(x):
    return lax.rsqrt(jnp.mean(x * x, axis=-1, keepdims=True) + RMS_EPS)


def _softplus(z):
    return jnp.maximum(z, 0.0) + jnp.log1p(jnp.exp(-jnp.abs(z)))


def _dot(a, b, dims=NN_DIMS):
    return lax.dot_general(a, b, dims, preferred_element_type=F32)


def _split_bf16(x, parts):
    out = []
    rem = x
    for _ in range(parts):
        p = rem.astype(BF16)
        out.append(p)
        rem = rem - p.astype(F32)
    return out


def _dot_f32(a, b, dims=NN_DIMS, passes=3):
    if passes == 1:
        return _dot(a.astype(BF16), b.astype(BF16), dims)
    a_p = _split_bf16(a, 2 if passes == 3 else 3)
    b_p = _split_bf16(b, 2 if passes == 3 else 3)
    acc = None
    for ia, ap in enumerate(a_p):
        for ib, bp in enumerate(b_p):
            if ia + ib >= len(a_p):
                continue
            t = _dot(ap, bp, dims)
            acc = t if acc is None else acc + t
    return acc


def _ffn_kernel(x_ref, g_ref, wg_ref, wu_ref, wd_ref, *rest, final_norm):
    if final_norm:
        fin_ref, o_ref, h_ref = rest
    else:
        o_ref, h_ref = rest
    j = pl.program_id(1)

    @pl.when(j == 0)
    def _():
        x = x_ref[...]
        h_ref[...] = (x * _rms_scale(x) * g_ref[...]).astype(BF16)
        o_ref[...] = jnp.zeros_like(o_ref)

    h = h_ref[...]
    gate = _dot(h, wg_ref[...])
    up = _dot(h, wu_ref[...])
    act = (gate * jax.nn.sigmoid(gate) * up).astype(BF16)
    o_ref[...] += _dot(act, wd_ref[...])

    @pl.when(j == pl.num_programs(1) - 1)
    def _():
        y = x_ref[...] + 0.5 * o_ref[...]
        if final_norm:
            y = y * _rms_scale(y) * fin_ref[...]
        o_ref[...] = y


def _ffn(x, g, w_up, w_down, fin=None, *, tm=512, tf=512):
    m, d = x.shape
    f = w_down.shape[0]
    tm = min(tm, m)
    nf = f // tf
    in_specs = [
        pl.BlockSpec((tm, d), lambda i, j: (i, 0)),
        pl.BlockSpec((1, d), lambda i, j: (0, 0)),
        pl.BlockSpec((d, tf), lambda i, j: (0, j)),
        pl.BlockSpec((d, tf), lambda i, j: (0, j + nf)),
        pl.BlockSpec((tf, d), lambda i, j: (j, 0)),
    ]
    args = [x, g.reshape(1, d), w_up, w_up, w_down]
    if fin is not None:
        in_specs.append(pl.BlockSpec((1, d), lambda i, j: (0, 0)))
        args.append(fin.reshape(1, d))
    return pl.pallas_call(
        functools.partial(_ffn_kernel, final_norm=fin is not None),
        grid=(m // tm, nf),
        in_specs=in_specs,
        out_specs=pl.BlockSpec((tm, d), lambda i, j: (i, 0)),
        out_shape=jax.ShapeDtypeStruct((m, d), F32),
        scratch_shapes=[pltpu.VMEM((tm, d), BF16)],
        compiler_params=_cparams("parallel", "arbitrary"),
        name="ffn",
    )(*args)


def _mm_kernel(*refs, n_b, n_row, n_tile, n_out, rmsnorm, epilogue):
    pos = 0
    a_ref = refs[pos]; pos += 1
    if rmsnorm:
        g_ref = refs[pos]; pos += 1
    b_refs = refs[pos:pos + n_b]; pos += n_b
    row_refs = refs[pos:pos + n_row]; pos += n_row
    tile_refs = refs[pos:pos + n_tile]; pos += n_tile
    out_refs = refs[pos:pos + n_out]; pos += n_out
    if rmsnorm:
        h_ref = refs[pos]

        @pl.when(pl.program_id(1) == 0)
        def _():
            x = a_ref[...]
            h_ref[...] = (x * _rms_scale(x) * g_ref[...]).astype(BF16)

        a = h_ref[...]
    else:
        a = a_ref[...]
    accs = [_dot(a, b_ref[...]) for b_ref in b_refs]
    outs = epilogue(accs, [r[...] for r in row_refs], [t[...] for t in tile_refs])
    for o_ref, o in zip(out_refs, outs):
        o_ref[...] = o.astype(o_ref.dtype)


def _mm(a, bs, epilogue, out_dtypes, *, n, rows=(), tiles=(), gain=None, tm=512, tn=512):
    m, k = a.shape
    tm = min(tm, m)
    tn = min(tn, n)
    rmsnorm = gain is not None
    in_specs = [pl.BlockSpec((tm, k), lambda i, j: (i, 0))]
    args = [a]
    if rmsnorm:
        in_specs.append(pl.BlockSpec((1, k), lambda i, j: (0, 0)))
        args.append(gain.reshape(1, k))
    for b, off in bs:
        ob = off // tn
        in_specs.append(pl.BlockSpec((k, tn), lambda i, j, ob=ob: (0, j + ob)))
        args.append(b)
    for r in rows:
        if r.shape[1] == tn and n != tn:
            in_specs.append(pl.BlockSpec((1, tn), lambda i, j: (0, 0)))
        else:
            in_specs.append(pl.BlockSpec((1, tn), lambda i, j: (0, j)))
        args.append(r)
    for t in tiles:
        in_specs.append(pl.BlockSpec((tm, tn), lambda i, j: (i, j)))
        args.append(t)
    out_shape = [jax.ShapeDtypeStruct((m, n), dt) for dt in out_dtypes]
    out_specs = [pl.BlockSpec((tm, tn), lambda i, j: (i, j)) for _ in out_dtypes]
    kern = functools.partial(
        _mm_kernel, n_b=len(bs), n_row=len(rows), n_tile=len(tiles), n_out=len(out_dtypes),
        rmsnorm=rmsnorm, epilogue=epilogue)
    return pl.pallas_call(
        kern,
        grid=(m // tm, n // tn),
        in_specs=in_specs,
        out_specs=out_specs,
        out_shape=out_shape,
        scratch_shapes=[pltpu.VMEM((tm, k), BF16)] if rmsnorm else [],
        compiler_params=_cparams("parallel", "arbitrary"),
        name="proj",
    )(*args)


def _fox_qkvg_epilogue(accs, rows, tiles):
    gq, gk = rows
    q, k, v, g = accs
    tn = q.shape[1]

    def head_norm(t, gain):
        parts = []
        for hh in range(tn // FOX_HEAD_DIM):
            sl = t[:, hh * FOX_HEAD_DIM:(hh + 1) * FOX_HEAD_DIM]
            parts.append(sl * _rms_scale(sl))
        return jnp.concatenate(parts, axis=1) * gain

    return head_norm(q, gq), head_norm(k, gk), v, jax.nn.sigmoid(g)


def _fox_forget_kernel(x_ref, g_ref, wf_ref, bf_ref, tri_ref, c_ref, carry_ref):
    i = pl.program_id(0)

    @pl.when(i == 0)
    def _():
        carry_ref[...] = jnp.zeros_like(carry_ref)

    x = x_ref[...]
    h = (x * _rms_scale(x) * g_ref[...]).astype(BF16)
    f_logit = _dot(wf_ref[...], h, NT_DIMS) + bf_ref[...]
    log_f = -_softplus(-f_logit) * LOG2_E
    tri = tri_ref[...]
    local = None
    for part in _split_bf16(log_f, 3):
        t = _dot(part, tri)
        local = t if local is None else local + t
    c = local + carry_ref[...]
    c_ref[...] = c
    carry_ref[...] = c[:, -1:]


def _fox_forget(x, g, wf_t, b_f, *, tm=512):
    m, d = x.shape
    nh = wf_t.shape[0]
    tm = min(tm, m)
    tri = jnp.triu(jnp.ones((tm, tm), F32)).astype(BF16)
    return pl.pallas_call(
        _fox_forget_kernel,
        grid=(m // tm,),
        in_specs=[
            pl.BlockSpec((tm, d), lambda i: (i, 0)),
            pl.BlockSpec((1, d), lambda i: (0, 0)),
            pl.BlockSpec((nh, d), lambda i: (0, 0)),
            pl.BlockSpec((nh, 1), lambda i: (0, 0)),
            pl.BlockSpec((tm, tm), lambda i: (0, 0)),
        ],
        out_specs=pl.BlockSpec((nh, tm), lambda i: (0, i)),
        out_shape=jax.ShapeDtypeStruct((nh, m), F32),
        scratch_shapes=[pltpu.VMEM((nh, 1), F32)],
        compiler_params=_cparams("arbitrary"),
        name="fox_forget",
    )(x, g.reshape(1, d), wf_t, b_f.reshape(nh, 1), tri)


def _fox_attn_kernel(q_ref, k_ref, v_ref, cq_ref, ck_ref, g_ref, o_ref, m_ref, acc_ref,
                     *, tq, tk, rb):
    i = pl.program_id(1)
    n_rb = tq // rb
    kpq = tq // tk
    m_ref[...] = jnp.full_like(m_ref, NEG_BIG)
    acc_ref[...] = jnp.zeros_like(acc_ref)
    c_first = cq_ref[0, 0][:, 0:1]
    dh = q_ref.shape[1]

    def step(j, row_blocks):
        ks = pl.multiple_of(j * tk, tk)
        k = k_ref[pl.ds(ks, tk), :]
        v = v_ref[pl.ds(ks, tk), :]
        v_aug = jnp.concatenate([v, jnp.ones_like(v)], axis=1)
        bias = c_first - ck_ref[0, j]
        for r, mask in row_blocks:
            rows = pl.ds(r * rb, rb)
            s = _dot(q_ref[rows, :], k, NT_DIMS) + bias
            if mask is not None:
                s = jnp.where(mask, s, NEG_BIG)
            m_prev = m_ref[rows, :]
            m_new = jnp.maximum(m_prev, jnp.max(s, axis=-1, keepdims=True))
            alpha = jnp.exp2(m_prev - m_new)
            pmat = jnp.exp2(s - pltpu.repeat(m_new, tk // dh, axis=1))
            acc_ref[rows, :] = (pltpu.repeat(alpha, 2, axis=1) * acc_ref[rows, :]
                                + _dot(pmat.astype(BF16), v_aug))
            m_ref[rows, :] = m_new

    full = [(r, None) for r in range(n_rb)]

    def body(j, carry):
        step(j, full)
        return carry

    lax.fori_loop(0, i * kpq, body, 0)
    for jj in range(kpq):
        blocks = []
        for r in range(n_rb):
            r0, r1 = r * rb, (r + 1) * rb - 1
            c0, c1 = jj * tk, (jj + 1) * tk - 1
            if r1 < c0:
                continue
            if r0 >= c1:
                blocks.append((r, None))
            else:
                row = r0 + lax.broadcasted_iota(jnp.int32, (rb, tk), 0)
                col = c0 + lax.broadcasted_iota(jnp.int32, (rb, tk), 1)
                blocks.append((r, col <= row))
        step(i * kpq + jj, blocks)
    o = acc_ref[:, :dh] / acc_ref[:, dh:]
    o_ref[...] = (o * g_ref[...].astype(F32)).astype(o_ref.dtype)


def _fox_attention(q, k, v, c, gate, *, tq=1024, tk=512, rb=256):
    m, d = q.shape
    nh = d // FOX_HEAD_DIM
    tq = min(tq, m)
    tk = min(tk, tq)
    rb = min(rb, tq)
    dh = FOX_HEAD_DIM
    cq = c.reshape(nh, m // tq, 1, tq)
    ck = c.reshape(nh, m // tk, 1, tk)
    return pl.pallas_call(
        functools.partial(_fox_attn_kernel, tq=tq, tk=tk, rb=rb),
        grid=(nh, m // tq),
        in_specs=[
            pl.BlockSpec((tq, dh), lambda h, i: (i, h)),
            pl.BlockSpec((m, dh), lambda h, i: (0, h)),
            pl.BlockSpec((m, dh), lambda h, i: (0, h)),
            pl.BlockSpec((1, 1, 1, tq), lambda h, i: (h, i, 0, 0)),
            pl.BlockSpec((1, m // tk, 1, tk), lambda h, i: (h, 0, 0, 0)),
            pl.BlockSpec((tq, dh), lambda h, i: (i, h)),
        ],
        out_specs=pl.BlockSpec((tq, dh), lambda h, i: (i, h)),
        out_shape=jax.ShapeDtypeStruct((m, d), BF16),
        scratch_shapes=[
            pltpu.VMEM((tq, dh), F32),
            pltpu.VMEM((tq, 2 * dh), F32),
        ],
        compiler_params=_cparams("parallel", "arbitrary"),
        name="fox_attn",
    )(q, k, v, cq, ck, gate)


def _fox_layer(x, norm_g, w_in, b_f, qk_gain, w_out):
    m, d = x.shape
    nh = d // FOX_HEAD_DIM
    w_in_b = w_in.astype(BF16)
    tn = 512
    scale = FOX_HEAD_DIM ** -0.5 * LOG2_E
    gq = jnp.tile(qk_gain[0] * scale, tn // FOX_HEAD_DIM).reshape(1, tn)
    gk = jnp.tile(qk_gain[1], tn // FOX_HEAD_DIM).reshape(1, tn)
    q, k, v, gate = _mm(
        x, [(w_in_b, 0), (w_in_b, d), (w_in_b, 2 * d), (w_in_b, 3 * d)],
        _fox_qkvg_epilogue, [BF16, BF16, BF16, BF16], n=d, rows=[gq, gk], gain=norm_g, tn=tn)
    wf_t = w_in[:, 4 * d:].T.astype(BF16)
    c = _fox_forget(x, norm_g, wf_t, b_f)
    og = _fox_attention(q, k, v, c, gate)
    (x_new,) = _mm(og, [(w_out.astype(BF16), 0)],
                   lambda accs, rows, tiles: (tiles[0] + accs[0],), [F32], n=d, tiles=[x])
    return x_new


def _rwkv_prep_kernel(x_ref, xp_ref, g_ref, mu_ref, *rest):
    out_refs = rest[:6]
    hbuf = rest[6]
    i = pl.program_id(0)
    tm = x_ref.shape[0]
    x = x_ref[...]
    g = g_ref[...]
    h = x * _rms_scale(x) * g
    xp = xp_ref[...][7:8, :]
    hp = xp * _rms_scale(xp) * g
    hp = jnp.where(i == 0, jnp.zeros_like(hp), hp)
    hbuf[pl.ds(8, tm), :] = h
    hbuf[pl.ds(7, 1), :] = hp
    xx = hbuf[pl.ds(7, tm), :] - h
    mu = mu_ref[...]
    for n, o_ref in enumerate(out_refs):
        o_ref[...] = (h + xx * mu[n:n + 1, :]).astype(o_ref.dtype)


def _rwkv_prep(x, g, mu, *, tm=256):
    m, d = x.shape
    tm = min(tm, m)
    rb = tm // 8
    return pl.pallas_call(
        _rwkv_prep_kernel,
        grid=(m // tm,),
        in_specs=[
            pl.BlockSpec((tm, d), lambda i: (i, 0)),
            pl.BlockSpec((8, d), lambda i: (jnp.maximum(i * rb - 1, 0), 0)),
            pl.BlockSpec((1, d), lambda i: (0, 0)),
            pl.BlockSpec((8, d), lambda i: (0, 0)),
        ],
        out_specs=[pl.BlockSpec((tm, d), lambda i: (i, 0)) for _ in range(6)],
        out_shape=[jax.ShapeDtypeStruct((m, d), BF16) for _ in range(6)],
        scratch_shapes=[pltpu.VMEM((tm + 8, d), F32)],
        compiler_params=_cparams("parallel"),
        name="rwkv_prep",
    )(x, x, g.reshape(1, d), jnp.pad(mu, ((0, 2), (0, 0))))


def _lora_kernel(x_ref, w1_ref, w2_ref, b_ref, o_ref, *, mid_act, out_act):
    t = _dot(x_ref[...], w1_ref[...])
    t = mid_act(t).astype(BF16)
    y = _dot(t, w2_ref[...]) + b_ref[...]
    o_ref[...] = out_act(y).astype(o_ref.dtype)


def _lora(x, w1, w2, bias, mid_act, out_act, *, tm=512):
    m, d = x.shape
    r = w1.shape[1]
    n = w2.shape[1]
    tm = min(tm, m)
    return pl.pallas_call(
        functools.partial(_lora_kernel, mid_act=mid_act, out_act=out_act),
        grid=(m // tm,),
        in_specs=[
            pl.BlockSpec((tm, d), lambda i: (i, 0)),
            pl.BlockSpec((d, r), lambda i: (0, 0)),
            pl.BlockSpec((r, n), lambda i: (0, 0)),
            pl.BlockSpec((1, n), lambda i: (0, 0)),
        ],
        out_specs=pl.BlockSpec((tm, n), lambda i: (i, 0)),
        out_shape=jax.ShapeDtypeStruct((m, n), F32),
        compiler_params=_cparams("parallel"),
        name="rwkv_lora",
    )(x, w1.astype(BF16), w2.astype(BF16), bias.reshape(1, n))


def _tri_mask(n, strict):
    row = lax.broadcasted_iota(jnp.int32, (n, n), 0)
    col = lax.broadcasted_iota(jnp.int32, (n, n), 1)
    return (col < row) if strict else (col <= row)


def _bdot(a, b, spec, passes=3):
    ein = lambda x, y: jnp.einsum(spec, x, y, preferred_element_type=F32)
    if passes == 1:
        return ein(a.astype(BF16), b.astype(BF16))
    a_hi, a_lo = _split_bf16(a, 2)
    b_hi, b_lo = _split_bf16(b, 2)
    return ein(a_hi, b_hi) + ein(a_hi, b_lo) + ein(a_lo, b_hi)


def _rwkv_core_kernel(r_ref, k_ref, v_ref, ld_ref, a_ref, g_ref, kk_ref, ka_ref, rk_ref,
                      lnw_ref, lnb_ref, tri_ref, o_ref, st_ref, *, chunk):
    t_idx = pl.program_id(1)

    @pl.when(t_idx == 0)
    def _():
        st_ref[...] = jnp.zeros_like(st_ref)

    tt = r_ref.shape[0]
    c = chunk
    nc = tt // c
    n = RWKV_HEAD_DIM
    shape3 = (nc, c, LANES)
    r = r_ref[...].reshape(shape3)
    k = k_ref[...].reshape(shape3)
    v = v_ref[...].reshape(shape3)
    ld = ld_ref[...]
    a = a_ref[...].reshape(shape3)

    lane = lax.broadcasted_iota(jnp.int32, (1, 1, LANES), 2)
    head0 = lane < n
    m0 = head0.astype(F32)
    m1 = 1.0 - m0
    row_l = lax.broadcasted_iota(jnp.int32, (LANES, LANES), 0)
    col_l = lax.broadcasted_iota(jnp.int32, (LANES, LANES), 1)
    same_head = (row_l < n) == (col_l < n)
    seg_ones = same_head.astype(BF16)
    bd_mask = same_head.astype(F32)
    eye_l = (row_l == col_l).astype(F32)

    def seg_sum(x):
        acc = None
        for part in _split_bf16(x.reshape(tt, LANES), 3):
            t = _dot(part, seg_ones)
            acc = t if acc is None else acc + t
        return acc.reshape(shape3)

    both = lambda x: jnp.concatenate([x, x], axis=0)
    own = lambda x: jnp.where(head0, x[:nc], x[nc:])

    kk = k * kk_ref[...]
    kk = kk / jnp.maximum(jnp.sqrt(seg_sum(kk * kk)), RWKV_NORM_EPS)
    k2 = k * (1.0 + (a - 1.0) * ka_ref[...])
    av = -kk
    bv = kk * a
    gam = None
    tri = tri_ref[...]
    for part in _split_bf16(ld, 3):
        t = _dot(tri, part)
        gam = t if gam is None else gam + t
    gam = gam.reshape(shape3)
    ld = ld.reshape(shape3)
    g_end = gam[:, c - 1:c, :]
    at = av * jnp.exp(gam - ld)
    rt = r * jnp.exp(gam)
    e_neg = jnp.exp(-gam)
    bt = bv * e_neg
    kt = k2 * e_neg
    e_end = jnp.exp(g_end - gam)
    b_end = bv * e_end
    k_end = k2 * e_end

    lhs = both(jnp.concatenate([at, rt], axis=1))
    gb = _bdot(lhs, jnp.concatenate([bt * m0, bt * m1], axis=0), 'bqd,bkd->bqk')
    gk = _bdot(lhs, jnp.concatenate([kt * m0, kt * m1], axis=0), 'bqd,bkd->bqk')
    strict = _tri_mask(c, True)[None]
    incl = _tri_mask(c, False)[None]
    g_ab = jnp.where(strict, gb[:, :c], 0.0)
    g_rb = jnp.where(incl, gb[:, c:], 0.0)
    g_ak = jnp.where(strict, gk[:, :c], 0.0)
    g_rk = jnp.where(incl, gk[:, c:], 0.0)
    eye_c = (lax.broadcasted_iota(jnp.int32, (c, c), 0)
             == lax.broadcasted_iota(jnp.int32, (c, c), 1)).astype(F32)[None]
    tinv = eye_c + g_ab
    npow = g_ab
    for _ in range(int(math.log2(c)) - 1):
        npow = _bdot(npow, npow, 'bij,bjk->bik', passes=1)
        tinv = tinv + _bdot(tinv, npow, 'bij,bjk->bik', passes=1)
    v2 = both(v)
    av_v = own(_bdot(g_ak, v2, 'bts,bsl->btl'))
    pmat = own(_bdot(tinv, both(at), 'bts,bsl->btl'))
    qmat = own(_bdot(tinv, both(av_v), 'bts,bsl->btl'))
    m_mat = _bdot(b_end, pmat, 'bti,btj->bij') * bd_mask + eye_l * jnp.exp(g_end)
    n_mat = (_bdot(b_end, qmat, 'bti,btj->bij') + _bdot(k_end, v, 'bti,btj->bij')) * bd_mask
    o1 = rt + own(_bdot(g_rb, both(pmat), 'bts,bsl->btl'))
    o2 = own(_bdot(g_rb, both(qmat), 'bts,bsl->btl') + _bdot(g_rk, v2, 'bts,bsl->btl'))

    st = st_ref[...]
    outs = []
    for cc in range(nc):
        outs.append(_dot_f32(o1[cc], st) + o2[cc])
        st = _dot_f32(m_mat[cc], st) + n_mat[cc]
    st_ref[...] = st
    o = jnp.stack(outs, axis=0)

    inv_n = 1.0 / n
    mean = seg_sum(o) * inv_n
    cen = o - mean
    var = seg_sum(cen * cen) * inv_n
    y = cen * lax.rsqrt(var + RWKV_LN_EPS) * lnw_ref[...] + lnb_ref[...]
    bonus = seg_sum(r * k2 * rk_ref[...]) * v
    out = (y + bonus) * g_ref[...].reshape(shape3)
    o_ref[...] = out.reshape(tt, LANES).astype(o_ref.dtype)


def _rwkv_core(r, k, v, ld, a, g, k_k, k_a, r_k, ln_w, ln_b, *, tt=256):
    m, d = r.shape
    tt = min(tt, m)
    chunk = min(RWKV_CHUNK, tt)
    idx = jnp.arange(tt)
    tri = ((idx[:, None] >= idx[None, :]) & (idx[:, None] // chunk == idx[None, :] // chunk)).astype(BF16)
    seq = pl.BlockSpec((tt, LANES), lambda hp, t: (t, hp))
    par = pl.BlockSpec((1, LANES), lambda hp, t: (0, hp))
    return pl.pallas_call(
        functools.partial(_rwkv_core_kernel, chunk=chunk),
        grid=(d // LANES, m // tt),
        in_specs=[seq] * 6 + [par] * 5 + [pl.BlockSpec((tt, tt), lambda hp, t: (0, 0))],
        out_specs=seq,
        out_shape=jax.ShapeDtypeStruct((m, d), BF16),
        scratch_shapes=[pltpu.VMEM((LANES, LANES), F32)],
        compiler_params=_cparams("parallel", "arbitrary"),
        name="rwkv_core",
    )(r, k, v, ld, a, g, k_k.reshape(1, d), k_a.reshape(1, d), r_k.reshape(1, d),
      ln_w.reshape(1, d), ln_b.reshape(1, d), tri)


def _rwkv_layer(x, norm_g, mu, w_rkv, w0, w1, w2, a0, a1, a2, g1, g2, k_k, k_a, r_k, ln_w, ln_b, w_out):
    m, d = x.shape
    xr, xw, xk, xv, xa, xg = _rwkv_prep(x, norm_g, mu)
    w_rkv_b = w_rkv.astype(BF16)
    plain = lambda accs, rows, tiles: (accs[0],)
    (r,) = _mm(xr, [(w_rkv_b[0], 0)], plain, [F32], n=d)
    (k,) = _mm(xk, [(w_rkv_b[1], 0)], plain, [F32], n=d)
    (v,) = _mm(xv, [(w_rkv_b[2], 0)], plain, [F32], n=d)
    ident = lambda t: t
    ld = _lora(xw, w1, w2, w0, jnp.tanh, lambda y: -jnp.exp(-_softplus(-y) - 0.5))
    a = _lora(xa, a1, a2, a0, ident, jax.nn.sigmoid)
    g = _lora(xg, g1, g2, jnp.zeros((d,), F32), jax.nn.sigmoid, ident)
    y = _rwkv_core(r, k, v, ld, a, g, k_k, k_a, r_k, ln_w, ln_b)
    (x_new,) = _mm(y, [(w_out.astype(BF16), 0)],
                   lambda accs, rows, tiles: (tiles[0] + accs[0],), [F32], n=d, tiles=[x])
    return x_new


def _s5_core_kernel(u_ref, b_ref, a_ref, c_ref, d_ref, o_ref, hbuf, carry_ref, *, pad):
    t_idx = pl.program_id(1)
    tl = u_ref.shape[0]
    half = a_ref.shape[2] // 2

    @pl.when(t_idx == 0)
    def _():
        carry_ref[...] = jnp.zeros_like(carry_ref)
        hbuf[pl.ds(0, pad), :] = jnp.zeros((pad, hbuf.shape[1]), F32)

    u = u_ref[...]
    bu = _dot(u.astype(BF16), b_ref[0])
    a_row = a_ref[0]
    ar = a_row[:, :half]
    ai = a_row[:, half:]
    hr0 = carry_ref[0:1, :half]
    hi0 = carry_ref[0:1, half:]
    hbuf[pl.ds(pad, tl), :] = bu
    first = hbuf[pl.ds(pad, 1), :]
    hbuf[pl.ds(pad, 1), :] = first + jnp.concatenate(
        [ar * hr0 - ai * hi0, ar * hi0 + ai * hr0], axis=1)
    rc = S5_SCAN_ROWS
    s = 1
    while s < tl:
        def level(n, carry, s=s, ar=ar, ai=ai):
            t0 = pl.multiple_of(pad + tl - (n + 1) * rc, rc)
            xr = hbuf[pl.ds(t0, rc), pl.ds(0, half)]
            xi = hbuf[pl.ds(t0, rc), pl.ds(half, half)]
            if s >= 8:
                ts = pl.multiple_of(t0 - s, 8)
                sr = hbuf[pl.ds(ts, rc), pl.ds(0, half)]
                si = hbuf[pl.ds(ts, rc), pl.ds(half, half)]
            else:
                tp = pl.multiple_of(t0 - 8, 8)
                sr = pltpu.roll(jnp.concatenate([hbuf[pl.ds(tp, 8), pl.ds(0, half)], xr], axis=0), s, axis=0)[8:, :]
                si = pltpu.roll(jnp.concatenate([hbuf[pl.ds(tp, 8), pl.ds(half, half)], xi], axis=0), s, axis=0)[8:, :]
            hbuf[pl.ds(t0, rc), pl.ds(0, half)] = xr + ar * sr - ai * si
            hbuf[pl.ds(t0, rc), pl.ds(half, half)] = xi + ar * si + ai * sr
            return carry
        lax.fori_loop(0, tl // rc, level, 0)
        ar, ai = ar * ar - ai * ai, 2.0 * ar * ai
        s *= 2
    hfin = hbuf[pl.ds(pad, tl), :]
    carry_ref[0:1, :] = hfin[tl - 1:tl, :]
    y = _dot(hfin.astype(BF16), c_ref[0]) + d_ref[...] * u
    o_ref[...] = jax.nn.gelu(y).astype(o_ref.dtype)


def _s5_core(u, b_blk, a_blk, c_blk, d_skip, *, tl=256):
    m, d = u.shape
    tl = min(tl, m)
    pad = tl // 2 if tl >= 16 else 8
    nb = d // LANES
    width = b_blk.shape[2]
    return pl.pallas_call(
        functools.partial(_s5_core_kernel, pad=pad),
        grid=(nb, m // tl),
        in_specs=[
            pl.BlockSpec((tl, LANES), lambda gb, t: (t, gb)),
            pl.BlockSpec((1, LANES, width), lambda gb, t: (gb, 0, 0)),
            pl.BlockSpec((1, 1, width), lambda gb, t: (gb, 0, 0)),
            pl.BlockSpec((1, width, LANES), lambda gb, t: (gb, 0, 0)),
            pl.BlockSpec((1, LANES), lambda gb, t: (0, gb)),
        ],
        out_specs=pl.BlockSpec((tl, LANES), lambda gb, t: (t, gb)),
        out_shape=jax.ShapeDtypeStruct((m, d), BF16),
        scratch_shapes=[pltpu.VMEM((pad + tl, width), F32), pltpu.VMEM((8, width), F32)],
        compiler_params=_cparams("parallel", "arbitrary"),
        name="s5_core",
    )(u, b_blk, a_blk, c_blk, d_skip.reshape(1, d))


def _s5_tables(lam_re, lam_im, log_step, b_re, b_im, c_re, c_im):
    g, p = lam_re.shape
    q = b_re.shape[2]
    gpb = S5_GROUPS_PER_BLOCK
    nb = g // gpb
    lr = jnp.minimum(lam_re.astype(F32), S5_MAX_RE)
    li = lam_im.astype(F32)
    dt = jnp.exp(log_step.astype(F32))[:, None]
    mag = jnp.exp(lr * dt)
    abar_re, abar_im = mag * jnp.cos(li * dt), mag * jnp.sin(li * dt)
    den = lr * lr + li * li
    nr, ni = abar_re - 1.0, abar_im
    q_re, q_im = (nr * lr + ni * li) / den, (ni * lr - nr * li) / den
    br, bi = b_re.astype(F32), b_im.astype(F32)
    bbar_re = q_re[..., None] * br - q_im[..., None] * bi
    bbar_im = q_re[..., None] * bi + q_im[..., None] * br
    eye = jnp.eye(gpb, dtype=F32)

    def blockdiag_in(bb):
        t = bb.reshape(nb, gpb, p, q).transpose(0, 1, 3, 2)
        return jnp.einsum('ngqp,gh->ngqhp', t, eye).reshape(nb, gpb * q, gpb * p)

    def blockdiag_out(cc):
        t = cc.reshape(nb, gpb, q, p).transpose(0, 1, 3, 2)
        return jnp.einsum('ngpq,gh->ngphq', t, eye).reshape(nb, gpb * p, gpb * q)

    b_blk = jnp.concatenate([blockdiag_in(bbar_re), blockdiag_in(bbar_im)], axis=2).astype(BF16)
    c_blk = jnp.concatenate([blockdiag_out(c_re.astype(F32)), -blockdiag_out(c_im.astype(F32))],
                            axis=1).astype(BF16)
    a_blk = jnp.concatenate([abar_re.reshape(nb, 1, gpb * p), abar_im.reshape(nb, 1, gpb * p)], axis=2)
    return b_blk, a_blk, c_blk


def _s5_layer(x, norm_g, w_in, lam_re, lam_im, log_step, b_re, b_im, c_re, c_im, d_skip, w_out):
    m, d = x.shape
    (u,) = _mm(x, [(w_in.astype(BF16), 0)], lambda accs, rows, tiles: (accs[0],), [F32],
               n=d, gain=norm_g)
    b_blk, a_blk, c_blk = _s5_tables(lam_re, lam_im, log_step, b_re, b_im, c_re, c_im)
    y = _s5_core(u, b_blk, a_blk, c_blk, d_skip)
    w_out_b = w_out.astype(BF16)
    (x_new,) = _mm(y, [(w_out_b, 0), (w_out_b, d)],
                   lambda accs, rows, tiles: (tiles[0] + accs[0] * jax.nn.sigmoid(accs[1]),),
                   [F32], n=d, tiles=[x])
    return x_new


def kernel(x, norm_w, ffn_w_up, ffn_w_down, fox_w_in, fox_b_f, fox_qk_gain, fox_w_out, rwkv_mu, rwkv_w_rkv, rwkv_w0, rwkv_w1, rwkv_w2, rwkv_a0, rwkv_a1, rwkv_a2, rwkv_g1, rwkv_g2, rwkv_k_k, rwkv_k_a, rwkv_r_k, rwkv_ln_w, rwkv_ln_b, rwkv_w_out, s5_w_in, s5_lam_re, s5_lam_im, s5_log_step, s5_b_re, s5_b_im, s5_c_re, s5_c_im, s5_d, s5_w_out, final_norm):
    bsz, s, d = x.shape
    depth = norm_w.shape[0]
    outs = []
    for b in range(bsz):
        xb = x[b]
        ia = ib = ic = 0
        for i in range(depth):
            xb = _ffn(xb, norm_w[i, 0], ffn_w_up[i, 0].astype(BF16), ffn_w_down[i, 0].astype(BF16))
            mixer = i % 3
            if mixer == 0:
                xb = _fox_layer(xb, norm_w[i, 1], fox_w_in[ia], fox_b_f[ia], fox_qk_gain[ia], fox_w_out[ia])
                ia += 1
            elif mixer == 1:
                xb = _rwkv_layer(xb, norm_w[i, 1], rwkv_mu[ib], rwkv_w_rkv[ib], rwkv_w0[ib], rwkv_w1[ib],
                                 rwkv_w2[ib], rwkv_a0[ib], rwkv_a1[ib], rwkv_a2[ib], rwkv_g1[ib], rwkv_g2[ib],
                                 rwkv_k_k[ib], rwkv_k_a[ib], rwkv_r_k[ib].reshape(-1), rwkv_ln_w[ib],
                                 rwkv_ln_b[ib], rwkv_w_out[ib])
                ib += 1
            else:
                xb = _s5_layer(xb, norm_w[i, 1], s5_w_in[ic], s5_lam_re[ic], s5_lam_im[ic], s5_log_step[ic],
                               s5_b_re[ic], s5_b_im[ic], s5_c_re[ic], s5_c_im[ic], s5_d[ic], s5_w_out[ic])
                ic += 1
            fin = final_norm if i == depth - 1 else None
            xb = _ffn(xb, norm_w[i, 2], ffn_w_up[i, 1].astype(BF16), ffn_w_down[i, 1].astype(BF16), fin)
        outs.append(xb)
    return jnp.stack(outs, axis=0)
```

```python
import functools
import math

import jax
import jax.numpy as jnp
from jax import lax
from jax.experimental import pallas as pl
from jax.experimental.pallas import tpu as pltpu

F32 = jnp.float32
BF16 = jnp.bfloat16

V7X_VMEM_BYTES = 64 * 1024 * 1024
VMEM_LIMIT_BYTES = V7X_VMEM_BYTES - 8 * 1024 * 1024
LANES = 128

RMS_EPS = 1e-6
FOX_HEAD_DIM = 128
RWKV_HEAD_DIM = 64
RWKV_LN_EPS = 64e-5
RWKV_NORM_EPS = 1e-12
RWKV_CHUNK = 64
S5_GROUP = 16
S5_STATE = 64
S5_MAX_RE = -1e-4
S5_GROUPS_PER_BLOCK = LANES // S5_GROUP
NEG_BIG = -1e30
LOG2_E = math.log2(math.e)
S5_SCAN_ROWS = 32
S5_LOCAL_SHIFTS = (1, 2, 4)

NT_DIMS = (((1,), (1,)), ((), ()))
TN_DIMS = (((0,), (0,)), ((), ()))
NN_DIMS = (((1,), (0,)), ((), ()))


def _cparams(*sem, vmem_limit_bytes=VMEM_LIMIT_BYTES):
    return pltpu.CompilerParams(dimension_semantics=sem, vmem_limit_bytes=vmem_limit_bytes)


def _rms_scale(x):
    return lax.rsqrt(jnp.mean(x * x, axis=-1, keepdims=True) + RMS_EPS)


def _softplus(z):
    return jnp.maximum(z, 0.0) + jnp.log1p(jnp.exp(-jnp.abs(z)))


def _dot(a, b, dims=NN_DIMS):
    return lax.dot_general(a, b, dims, preferred_element_type=F32)


def _split_bf16(x, parts):
    out = []
    rem = x
    for _ in range(parts):
        p = rem.astype(BF16)
        out.append(p)
        rem = rem - p.astype(F32)
    return out


def _dot_f32(a, b, dims=NN_DIMS, passes=3):
    if passes == 1:
        return _dot(a.astype(BF16), b.astype(BF16), dims)
    a_p = _split_bf16(a, 2 if passes == 3 else 3)
    b_p = _split_bf16(b, 2 if passes == 3 else 3)
    acc = None
    for ia, ap in enumerate(a_p):
        for ib, bp in enumerate(b_p):
            if ia + ib >= len(a_p):
                continue
            t = _dot(ap, bp, dims)
            acc = t if acc is None else acc + t
    return acc


def _ffn_kernel(x_ref, g_ref, wg_ref, wu_ref, wd_ref, *rest, final_norm):
    if final_norm:
        fin_ref, o_ref, h_ref = rest
    else:
        o_ref, h_ref = rest
    j = pl.program_id(1)

    @pl.when(j == 0)
    def _():
        x = x_ref[...]
        h_ref[...] = (x * _rms_scale(x) * g_ref[...]).astype(BF16)
        o_ref[...] = jnp.zeros_like(o_ref)

    h = h_ref[...]
    gate = _dot(h, wg_ref[...])
    up = _dot(h, wu_ref[...])
    act = (gate * jax.nn.sigmoid(gate) * up).astype(BF16)
    o_ref[...] += _dot(act, wd_ref[...])

    @pl.when(j == pl.num_programs(1) - 1)
    def _():
        y = x_ref[...] + 0.5 * o_ref[...]
        if final_norm:
            y = y * _rms_scale(y) * fin_ref[...]
        o_ref[...] = y


def _ffn(x, g, w_up, w_down, fin=None, *, tm=512, tf=512):
    m, d = x.shape
    f = w_down.shape[0]
    tm = min(tm, m)
    nf = f // tf
    in_specs = [
        pl.BlockSpec((tm, d), lambda i, j: (i, 0)),
        pl.BlockSpec((1, d), lambda i, j: (0, 0)),
        pl.BlockSpec((d, tf), lambda i, j: (0, j)),
        pl.BlockSpec((d, tf), lambda i, j: (0, j + nf)),
        pl.BlockSpec((tf, d), lambda i, j: (j, 0)),
    ]
    args = [x, g.reshape(1, d), w_up, w_up, w_down]
    if fin is not None:
        in_specs.append(pl.BlockSpec((1, d), lambda i, j: (0, 0)))
        args.append(fin.reshape(1, d))
    return pl.pallas_call(
        functools.partial(_ffn_kernel, final_norm=fin is not None),
        grid=(m // tm, nf),
        in_specs=in_specs,
        out_specs=pl.BlockSpec((tm, d), lambda i, j: (i, 0)),
        out_shape=jax.ShapeDtypeStruct((m, d), F32),
        scratch_shapes=[pltpu.VMEM((tm, d), BF16)],
        compiler_params=_cparams("parallel", "arbitrary"),
        name="ffn",
    )(*args)


def _mm_kernel(*refs, n_b, n_row, n_tile, n_out, rmsnorm, epilogue):
    pos = 0
    a_ref = refs[pos]; pos += 1
    if rmsnorm:
        g_ref = refs[pos]; pos += 1
    b_refs = refs[pos:pos + n_b]; pos += n_b
    row_refs = refs[pos:pos + n_row]; pos += n_row
    tile_refs = refs[pos:pos + n_tile]; pos += n_tile
    out_refs = refs[pos:pos + n_out]; pos += n_out
    if rmsnorm:
        h_ref = refs[pos]

        @pl.when(pl.program_id(1) == 0)
        def _():
            x = a_ref[...]
            h_ref[...] = (x * _rms_scale(x) * g_ref[...]).astype(BF16)

        a = h_ref[...]
    else:
        a = a_ref[...]
    accs = [_dot(a, b_ref[...]) for b_ref in b_refs]
    outs = epilogue(accs, [r[...] for r in row_refs], [t[...] for t in tile_refs])
    for o_ref, o in zip(out_refs, outs):
        o_ref[...] = o.astype(o_ref.dtype)


def _mm(a, bs, epilogue, out_dtypes, *, n, rows=(), tiles=(), gain=None, tm=512, tn=512):
    m, k = a.shape
    tm = min(tm, m)
    tn = min(tn, n)
    rmsnorm = gain is not None
    in_specs = [pl.BlockSpec((tm, k), lambda i, j: (i, 0))]
    args = [a]
    if rmsnorm:
        in_specs.append(pl.BlockSpec((1, k), lambda i, j: (0, 0)))
        args.append(gain.reshape(1, k))
    for b, off in bs:
        ob = off // tn
        in_specs.append(pl.BlockSpec((k, tn), lambda i, j, ob=ob: (0, j + ob)))
        args.append(b)
    for r in rows:
        if r.shape[1] == tn and n != tn:
            in_specs.append(pl.BlockSpec((1, tn), lambda i, j: (0, 0)))
        else:
            in_specs.append(pl.BlockSpec((1, tn), lambda i, j: (0, j)))
        args.append(r)
    for t in tiles:
        in_specs.append(pl.BlockSpec((tm, tn), lambda i, j: (i, j)))
        args.append(t)
    out_shape = [jax.ShapeDtypeStruct((m, n), dt) for dt in out_dtypes]
    out_specs = [pl.BlockSpec((tm, tn), lambda i, j: (i, j)) for _ in out_dtypes]
    kern = functools.partial(
        _mm_kernel, n_b=len(bs), n_row=len(rows), n_tile=len(tiles), n_out=len(out_dtypes),
        rmsnorm=rmsnorm, epilogue=epilogue)
    return pl.pallas_call(
        kern,
        grid=(m // tm, n // tn),
        in_specs=in_specs,
        out_specs=out_specs,
        out_shape=out_shape,
        scratch_shapes=[pltpu.VMEM((tm, k), BF16)] if rmsnorm else [],
        compiler_params=_cparams("parallel", "arbitrary"),
        name="proj",
    )(*args)


def _fox_qkvg_epilogue(accs, rows, tiles):
    gq, gk = rows
    q, k, v, g = accs
    tn = q.shape[1]

    def head_norm(t, gain):
        parts = []
        for hh in range(tn // FOX_HEAD_DIM):
            sl = t[:, hh * FOX_HEAD_DIM:(hh + 1) * FOX_HEAD_DIM]
            parts.append(sl * _rms_scale(sl))
        return jnp.concatenate(parts, axis=1) * gain

    return head_norm(q, gq), head_norm(k, gk), v, jax.nn.sigmoid(g)


def _fox_forget_kernel(x_ref, g_ref, wf_ref, bf_ref, tri_ref, c_ref, carry_ref):
    i = pl.program_id(0)

    @pl.when(i == 0)
    def _():
        carry_ref[...] = jnp.zeros_like(carry_ref)

    x = x_ref[...]
    h = (x * _rms_scale(x) * g_ref[...]).astype(BF16)
    f_logit = _dot(wf_ref[...], h, NT_DIMS) + bf_ref[...]
    log_f = -_softplus(-f_logit) * LOG2_E
    tri = tri_ref[...]
    local = None
    for part in _split_bf16(log_f, 3):
        t = _dot(part, tri)
        local = t if local is None else local + t
    c = local + carry_ref[...]
    c_ref[...] = c
    carry_ref[...] = c[:, -1:]


def _fox_forget(x, g, wf_t, b_f, *, tm=512):
    m, d = x.shape
    nh = wf_t.shape[0]
    tm = min(tm, m)
    tri = jnp.triu(jnp.ones((tm, tm), F32)).astype(BF16)
    return pl.pallas_call(
        _fox_forget_kernel,
        grid=(m // tm,),
        in_specs=[
            pl.BlockSpec((tm, d), lambda i: (i, 0)),
            pl.BlockSpec((1, d), lambda i: (0, 0)),
            pl.BlockSpec((nh, d), lambda i: (0, 0)),
            pl.BlockSpec((nh, 1), lambda i: (0, 0)),
            pl.BlockSpec((tm, tm), lambda i: (0, 0)),
        ],
        out_specs=pl.BlockSpec((nh, tm), lambda i: (0, i)),
        out_shape=jax.ShapeDtypeStruct((nh, m), F32),
        scratch_shapes=[pltpu.VMEM((nh, 1), F32)],
        compiler_params=_cparams("arbitrary"),
        name="fox_forget",
    )(x, g.reshape(1, d), wf_t, b_f.reshape(nh, 1), tri)


def _fox_attn_kernel(q_ref, k_ref, v_ref, cq_ref, ck_ref, g_ref, o_ref, m_ref, acc_ref,
                     *, tq, tk, rb):
    i = pl.program_id(1)
    n_rb = tq // rb
    kpq = tq // tk
    m_ref[...] = jnp.full_like(m_ref, NEG_BIG)
    acc_ref[...] = jnp.zeros_like(acc_ref)
    c_first = cq_ref[0, 0][:, 0:1]
    dh = q_ref.shape[1]

    def step(j, row_blocks):
        ks = pl.multiple_of(j * tk, tk)
        k = k_ref[pl.ds(ks, tk), :]
        v = v_ref[pl.ds(ks, tk), :]
        v_aug = jnp.concatenate([v, jnp.ones_like(v)], axis=1)
        bias = c_first - ck_ref[0, j]
        for r, mask in row_blocks:
            rows = pl.ds(r * rb, rb)
            s = _dot(q_ref[rows, :], k, NT_DIMS) + bias
            if mask is not None:
                s = jnp.where(mask, s, NEG_BIG)
            m_prev = m_ref[rows, :]
            m_new = jnp.maximum(m_prev, jnp.max(s, axis=-1, keepdims=True))
            alpha = jnp.exp2(m_prev - m_new)
            pmat = jnp.exp2(s - jnp.concatenate([m_new] * (tk // dh), axis=1))
            acc_ref[rows, :] = (jnp.concatenate([alpha, alpha], axis=1) * acc_ref[rows, :]
                                + _dot(pmat.astype(BF16), v_aug))
            m_ref[rows, :] = m_new

    full = [(r, None) for r in range(n_rb)]

    def body(jo, carry):
        for jj in range(kpq):
            step(jo * kpq + jj, full)
        return carry

    lax.fori_loop(0, i, body, 0)
    for jj in range(kpq):
        blocks = []
        for r in range(n_rb):
            r0, r1 = r * rb, (r + 1) * rb - 1
            c0, c1 = jj * tk, (jj + 1) * tk - 1
            if r1 < c0:
                continue
            if r0 >= c1:
                blocks.append((r, None))
            else:
                row = r0 + lax.broadcasted_iota(jnp.int32, (rb, tk), 0)
                col = c0 + lax.broadcasted_iota(jnp.int32, (rb, tk), 1)
                blocks.append((r, col <= row))
        step(i * kpq + jj, blocks)
    o = acc_ref[:, :dh] / acc_ref[:, dh:]
    o_ref[...] = (o * g_ref[...].astype(F32)).astype(o_ref.dtype)


def _fox_attention(q, k, v, c, gate, *, tq=1024, tk=512, rb=256):
    m, d = q.shape
    nh = d // FOX_HEAD_DIM
    tq = min(tq, m)
    tk = min(tk, tq)
    rb = min(rb, tq)
    dh = FOX_HEAD_DIM
    cq = c.reshape(nh, m // tq, 1, tq)
    ck = c.reshape(nh, m // tk, 1, tk)
    return pl.pallas_call(
        functools.partial(_fox_attn_kernel, tq=tq, tk=tk, rb=rb),
        grid=(nh, m // tq),
        in_specs=[
            pl.BlockSpec((tq, dh), lambda h, i: (i, h)),
            pl.BlockSpec((m, dh), lambda h, i: (0, h)),
            pl.BlockSpec((m, dh), lambda h, i: (0, h)),
            pl.BlockSpec((1, 1, 1, tq), lambda h, i: (h, i, 0, 0)),
            pl.BlockSpec((1, m // tk, 1, tk), lambda h, i: (h, 0, 0, 0)),
            pl.BlockSpec((tq, dh), lambda h, i: (i, h)),
        ],
        out_specs=pl.BlockSpec((tq, dh), lambda h, i: (i, h)),
        out_shape=jax.ShapeDtypeStruct((m, d), BF16),
        scratch_shapes=[
            pltpu.VMEM((tq, dh), F32),
            pltpu.VMEM((tq, 2 * dh), F32),
        ],
        compiler_params=_cparams("parallel", "arbitrary"),
        name="fox_attn",
    )(q, k, v, cq, ck, gate)


def _fox_layer(x, norm_g, w_in, b_f, qk_gain, w_out):
    m, d = x.shape
    nh = d // FOX_HEAD_DIM
    w_in_b = w_in.astype(BF16)
    tn = 512
    scale = FOX_HEAD_DIM ** -0.5 * LOG2_E
    gq = jnp.tile(qk_gain[0] * scale, tn // FOX_HEAD_DIM).reshape(1, tn)
    gk = jnp.tile(qk_gain[1], tn // FOX_HEAD_DIM).reshape(1, tn)
    q, k, v, gate = _mm(
        x, [(w_in_b, 0), (w_in_b, d), (w_in_b, 2 * d), (w_in_b, 3 * d)],
        _fox_qkvg_epilogue, [BF16, BF16, BF16, BF16], n=d, rows=[gq, gk], gain=norm_g, tn=tn)
    wf_t = w_in[:, 4 * d:].T.astype(BF16)
    c = _fox_forget(x, norm_g, wf_t, b_f)
    og = _fox_attention(q, k, v, c, gate)
    (x_new,) = _mm(og, [(w_out.astype(BF16), 0)],
                   lambda accs, rows, tiles: (tiles[0] + accs[0],), [F32], n=d, tiles=[x])
    return x_new


def _rwkv_prep_kernel(x_ref, xp_ref, g_ref, mu_ref, *rest):
    out_refs = rest[:6]
    hbuf = rest[6]
    i = pl.program_id(0)
    tm = x_ref.shape[0]
    x = x_ref[...]
    g = g_ref[...]
    h = x * _rms_scale(x) * g
    xp = xp_ref[...][7:8, :]
    hp = xp * _rms_scale(xp) * g
    hp = jnp.where(i == 0, jnp.zeros_like(hp), hp)
    hbuf[pl.ds(8, tm), :] = h
    hbuf[pl.ds(7, 1), :] = hp
    xx = hbuf[pl.ds(7, tm), :] - h
    mu = mu_ref[...]
    for n, o_ref in enumerate(out_refs):
        o_ref[...] = (h + xx * mu[n:n + 1, :]).astype(o_ref.dtype)


def _rwkv_prep(x, g, mu, *, tm=256):
    m, d = x.shape
    tm = min(tm, m)
    rb = tm // 8
    return pl.pallas_call(
        _rwkv_prep_kernel,
        grid=(m // tm,),
        in_specs=[
            pl.BlockSpec((tm, d), lambda i: (i, 0)),
            pl.BlockSpec((8, d), lambda i: (jnp.maximum(i * rb - 1, 0), 0)),
            pl.BlockSpec((1, d), lambda i: (0, 0)),
            pl.BlockSpec((8, d), lambda i: (0, 0)),
        ],
        out_specs=[pl.BlockSpec((tm, d), lambda i: (i, 0)) for _ in range(6)],
        out_shape=[jax.ShapeDtypeStruct((m, d), BF16) for _ in range(6)],
        scratch_shapes=[pltpu.VMEM((tm + 8, d), F32)],
        compiler_params=_cparams("parallel"),
        name="rwkv_prep",
    )(x, x, g.reshape(1, d), jnp.pad(mu, ((0, 2), (0, 0))))


def _lora_kernel(x_ref, w1_ref, w2_ref, b_ref, o_ref, *, mid_act, out_act):
    t = _dot(x_ref[...], w1_ref[...])
    t = mid_act(t).astype(BF16)
    y = _dot(t, w2_ref[...]) + b_ref[...]
    o_ref[...] = out_act(y).astype(o_ref.dtype)


def _lora(x, w1, w2, bias, mid_act, out_act, *, tm=512):
    m, d = x.shape
    r = w1.shape[1]
    n = w2.shape[1]
    tm = min(tm, m)
    return pl.pallas_call(
        functools.partial(_lora_kernel, mid_act=mid_act, out_act=out_act),
        grid=(m // tm,),
        in_specs=[
            pl.BlockSpec((tm, d), lambda i: (i, 0)),
            pl.BlockSpec((d, r), lambda i: (0, 0)),
            pl.BlockSpec((r, n), lambda i: (0, 0)),
            pl.BlockSpec((1, n), lambda i: (0, 0)),
        ],
        out_specs=pl.BlockSpec((tm, n), lambda i: (i, 0)),
        out_shape=jax.ShapeDtypeStruct((m, n), F32),
        compiler_params=_cparams("parallel"),
        name="rwkv_lora",
    )(x, w1.astype(BF16), w2.astype(BF16), bias.reshape(1, n))


def _tri_mask(n, strict):
    row = lax.broadcasted_iota(jnp.int32, (n, n), 0)
    col = lax.broadcasted_iota(jnp.int32, (n, n), 1)
    return (col < row) if strict else (col <= row)


def _bdot(a, b, spec, passes=3):
    ein = lambda x, y: jnp.einsum(spec, x, y, preferred_element_type=F32)
    if passes == 1:
        return ein(a.astype(BF16), b.astype(BF16))
    a_hi, a_lo = _split_bf16(a, 2)
    b_hi, b_lo = _split_bf16(b, 2)
    return ein(a_hi, b_hi) + ein(a_hi, b_lo) + ein(a_lo, b_hi)


def _rwkv_core_kernel(r_ref, k_ref, v_ref, ld_ref, a_ref, g_ref, kk_ref, ka_ref, rk_ref,
                      lnw_ref, lnb_ref, tri_ref, o_ref, st_ref, *, chunk):
    t_idx = pl.program_id(1)

    @pl.when(t_idx == 0)
    def _():
        st_ref[...] = jnp.zeros_like(st_ref)

    tt = r_ref.shape[0]
    c = chunk
    nc = tt // c
    n = RWKV_HEAD_DIM
    shape3 = (nc, c, LANES)
    r = r_ref[...].reshape(shape3)
    k = k_ref[...].reshape(shape3)
    v = v_ref[...].reshape(shape3)
    ld = ld_ref[...]
    a = a_ref[...].reshape(shape3)

    lane = lax.broadcasted_iota(jnp.int32, (1, 1, LANES), 2)
    head0 = lane < n
    m0 = head0.astype(F32)
    m1 = 1.0 - m0
    row_l = lax.broadcasted_iota(jnp.int32, (LANES, LANES), 0)
    col_l = lax.broadcasted_iota(jnp.int32, (LANES, LANES), 1)
    same_head = (row_l < n) == (col_l < n)
    seg_ones = same_head.astype(BF16)
    bd_mask = same_head.astype(F32)
    eye_l = (row_l == col_l).astype(F32)

    def seg_sum(x):
        acc = None
        for part in _split_bf16(x.reshape(tt, LANES), 3):
            t = _dot(part, seg_ones)
            acc = t if acc is None else acc + t
        return acc.reshape(shape3)

    both = lambda x: jnp.concatenate([x, x], axis=0)
    own = lambda x: jnp.where(head0, x[:nc], x[nc:])

    kk = k * kk_ref[...]
    kk = kk / jnp.maximum(jnp.sqrt(seg_sum(kk * kk)), RWKV_NORM_EPS)
    k2 = k * (1.0 + (a - 1.0) * ka_ref[...])
    av = -kk
    bv = kk * a
    gam = None
    tri = tri_ref[...]
    for part in _split_bf16(ld, 3):
        t = _dot(tri, part)
        gam = t if gam is None else gam + t
    gam = gam.reshape(shape3)
    ld = ld.reshape(shape3)
    g_end = gam[:, c - 1:c, :]
    at = av * jnp.exp(gam - ld)
    rt = r * jnp.exp(gam)
    e_neg = jnp.exp(-gam)
    bt = bv * e_neg
    kt = k2 * e_neg
    e_end = jnp.exp(g_end - gam)
    b_end = bv * e_end
    k_end = k2 * e_end

    lhs = both(jnp.concatenate([at, rt], axis=1))
    gb = _bdot(lhs, jnp.concatenate([bt * m0, bt * m1], axis=0), 'bqd,bkd->bqk')
    gk = _bdot(lhs, jnp.concatenate([kt * m0, kt * m1], axis=0), 'bqd,bkd->bqk')
    strict = _tri_mask(c, True)[None]
    incl = _tri_mask(c, False)[None]
    g_ab = jnp.where(strict, gb[:, :c], 0.0)
    g_rb = jnp.where(incl, gb[:, c:], 0.0)
    g_ak = jnp.where(strict, gk[:, :c], 0.0)
    g_rk = jnp.where(incl, gk[:, c:], 0.0)
    eye_c = (lax.broadcasted_iota(jnp.int32, (c, c), 0)
             == lax.broadcasted_iota(jnp.int32, (c, c), 1)).astype(F32)[None]
    tinv = eye_c + g_ab
    npow = g_ab
    for _ in range(int(math.log2(c)) - 1):
        npow = _bdot(npow, npow, 'bij,bjk->bik', passes=1)
        tinv = tinv + _bdot(tinv, npow, 'bij,bjk->bik', passes=1)
    apply = lambda w, x: _bdot(w, x, 'bts,bsl->btl', passes=1)
    outer = lambda x, y: _bdot(x, y, 'bti,btj->bij', passes=1)
    v2 = both(v)
    av_v = own(apply(g_ak, v2))
    pmat = own(apply(tinv, both(at)))
    qmat = own(apply(tinv, both(av_v)))
    m_mat = outer(b_end, pmat) * bd_mask + eye_l * jnp.exp(g_end)
    n_mat = (outer(b_end, qmat) + outer(k_end, v)) * bd_mask
    o1 = rt + own(apply(g_rb, both(pmat)))
    o2 = own(apply(g_rb, both(qmat)) + apply(g_rk, v2))

    st = st_ref[...]
    outs = []
    for cc in range(nc):
        outs.append(_dot_f32(o1[cc], st, passes=1) + o2[cc])
        st = _dot_f32(m_mat[cc], st, passes=1) + n_mat[cc]
    st_ref[...] = st
    o = jnp.stack(outs, axis=0)

    inv_n = 1.0 / n
    mean = seg_sum(o) * inv_n
    cen = o - mean
    var = seg_sum(cen * cen) * inv_n
    y = cen * lax.rsqrt(var + RWKV_LN_EPS) * lnw_ref[...] + lnb_ref[...]
    bonus = seg_sum(r * k2 * rk_ref[...]) * v
    out = (y + bonus) * g_ref[...].reshape(shape3)
    o_ref[...] = out.reshape(tt, LANES).astype(o_ref.dtype)


def _rwkv_core(r, k, v, ld, a, g, k_k, k_a, r_k, ln_w, ln_b, *, tt=512):
    m, d = r.shape
    tt = min(tt, m)
    chunk = min(RWKV_CHUNK, tt)
    idx = jnp.arange(tt)
    tri = ((idx[:, None] >= idx[None, :]) & (idx[:, None] // chunk == idx[None, :] // chunk)).astype(BF16)
    seq = pl.BlockSpec((tt, LANES), lambda hp, t: (t, hp))
    par = pl.BlockSpec((1, LANES), lambda hp, t: (0, hp))
    return pl.pallas_call(
        functools.partial(_rwkv_core_kernel, chunk=chunk),
        grid=(d // LANES, m // tt),
        in_specs=[seq] * 6 + [par] * 5 + [pl.BlockSpec((tt, tt), lambda hp, t: (0, 0))],
        out_specs=seq,
        out_shape=jax.ShapeDtypeStruct((m, d), BF16),
        scratch_shapes=[pltpu.VMEM((LANES, LANES), F32)],
        compiler_params=_cparams("parallel", "arbitrary"),
        name="rwkv_core",
    )(r, k, v, ld, a, g, k_k.reshape(1, d), k_a.reshape(1, d), r_k.reshape(1, d),
      ln_w.reshape(1, d), ln_b.reshape(1, d), tri)


def _rwkv_layer(x, norm_g, mu, w_rkv, w0, w1, w2, a0, a1, a2, g1, g2, k_k, k_a, r_k, ln_w, ln_b, w_out):
    m, d = x.shape
    xr, xw, xk, xv, xa, xg = _rwkv_prep(x, norm_g, mu)
    w_rkv_b = w_rkv.astype(BF16)
    plain = lambda accs, rows, tiles: (accs[0],)
    (r,) = _mm(xr, [(w_rkv_b[0], 0)], plain, [F32], n=d)
    (k,) = _mm(xk, [(w_rkv_b[1], 0)], plain, [F32], n=d)
    (v,) = _mm(xv, [(w_rkv_b[2], 0)], plain, [F32], n=d)
    ident = lambda t: t
    ld = _lora(xw, w1, w2, w0, jnp.tanh, lambda y: -jnp.exp(-_softplus(-y) - 0.5))
    a = _lora(xa, a1, a2, a0, ident, jax.nn.sigmoid)
    g = _lora(xg, g1, g2, jnp.zeros((d,), F32), jax.nn.sigmoid, ident)
    y = _rwkv_core(r, k, v, ld, a, g, k_k, k_a, r_k, ln_w, ln_b)
    (x_new,) = _mm(y, [(w_out.astype(BF16), 0)],
                   lambda accs, rows, tiles: (tiles[0] + accs[0],), [F32], n=d, tiles=[x])
    return x_new


def _s5_core_kernel(u_ref, b_ref, tab_ref, c_ref, d_ref, o_ref, hbuf, cbuf, hb16, carry_ref):
    t_idx = pl.program_id(1)
    tl = u_ref.shape[0]
    width = hbuf.shape[1]
    half = width // 2
    rc = tab_ref.shape[2]
    n_lvl = len(S5_LOCAL_SHIFTS)
    re = pl.ds(0, half)
    im = pl.ds(half, half)

    @pl.when(t_idx == 0)
    def _():
        carry_ref[...] = jnp.zeros_like(carry_ref)

    u = u_ref[...]
    hbuf[...] = _dot(u.astype(BF16), b_ref[0])

    def local(n, carry):
        rows = pl.ds(pl.multiple_of(n * rc, rc), rc)
        xr = hbuf[rows, re]
        xi = hbuf[rows, im]
        for lvl, s in enumerate(S5_LOCAL_SHIFTS):
            tr = tab_ref[0, lvl, :, re]
            ti = tab_ref[0, lvl, :, im]
            sr = pltpu.roll(xr, s, axis=0)
            si = pltpu.roll(xi, s, axis=0)
            xr, xi = xr + tr * sr - ti * si, xi + tr * si + ti * sr
        hbuf[rows, re] = xr
        hbuf[rows, im] = xi
        return carry

    lax.fori_loop(0, tl // rc, local, 0)

    a8r = tab_ref[0, n_lvl, pl.ds(7, 1), re]
    a8i = tab_ref[0, n_lvl, pl.ds(7, 1), im]
    cr = carry_ref[0:1, re]
    ci = carry_ref[0:1, im]
    for g in range(tl // 8):
        cbuf[pl.ds(8 * g, 8), re] = jnp.broadcast_to(cr, (8, half))
        cbuf[pl.ds(8 * g, 8), im] = jnp.broadcast_to(ci, (8, half))
        lr = hbuf[pl.ds(8 * g + 7, 1), re]
        li = hbuf[pl.ds(8 * g + 7, 1), im]
        cr, ci = lr + a8r * cr - a8i * ci, li + a8r * ci + a8i * cr
    carry_ref[0:1, re] = cr
    carry_ref[0:1, im] = ci

    def apply(n, carry):
        rows = pl.ds(pl.multiple_of(n * rc, rc), rc)
        pr = tab_ref[0, n_lvl, :, re]
        pi = tab_ref[0, n_lvl, :, im]
        br = cbuf[rows, re]
        bi = cbuf[rows, im]
        hb16[rows, re] = (hbuf[rows, re] + pr * br - pi * bi).astype(BF16)
        hb16[rows, im] = (hbuf[rows, im] + pr * bi + pi * br).astype(BF16)
        return carry

    lax.fori_loop(0, tl // rc, apply, 0)
    y = _dot(hb16[...], c_ref[0]) + d_ref[...] * u
    o_ref[...] = jax.nn.gelu(y).astype(o_ref.dtype)


def _s5_core(u, b_blk, tab, c_blk, d_skip, *, tl=256):
    m, d = u.shape
    tl = min(tl, m)
    nb = d // LANES
    width = b_blk.shape[2]
    n_tab, rc = tab.shape[1], tab.shape[2]
    return pl.pallas_call(
        _s5_core_kernel,
        grid=(nb, m // tl),
        in_specs=[
            pl.BlockSpec((tl, LANES), lambda gb, t: (t, gb)),
            pl.BlockSpec((1, LANES, width), lambda gb, t: (gb, 0, 0)),
            pl.BlockSpec((1, n_tab, rc, width), lambda gb, t: (gb, 0, 0, 0)),
            pl.BlockSpec((1, width, LANES), lambda gb, t: (gb, 0, 0)),
            pl.BlockSpec((1, LANES), lambda gb, t: (0, gb)),
        ],
        out_specs=pl.BlockSpec((tl, LANES), lambda gb, t: (t, gb)),
        out_shape=jax.ShapeDtypeStruct((m, d), BF16),
        scratch_shapes=[pltpu.VMEM((tl, width), F32), pltpu.VMEM((tl, width), F32),
                        pltpu.VMEM((tl, width), BF16), pltpu.VMEM((8, width), F32)],
        compiler_params=_cparams("parallel", "arbitrary"),
        name="s5_core",
    )(u, b_blk, tab, c_blk, d_skip.reshape(1, d))


def _s5_tables(lam_re, lam_im, log_step, b_re, b_im, c_re, c_im):
    g, p = lam_re.shape
    q = b_re.shape[2]
    gpb = S5_GROUPS_PER_BLOCK
    nb = g // gpb
    lr = jnp.minimum(lam_re.astype(F32), S5_MAX_RE)
    li = lam_im.astype(F32)
    dt = jnp.exp(log_step.astype(F32))[:, None]
    mag = jnp.exp(lr * dt)
    abar_re, abar_im = mag * jnp.cos(li * dt), mag * jnp.sin(li * dt)
    den = lr * lr + li * li
    nr, ni = abar_re - 1.0, abar_im
    q_re, q_im = (nr * lr + ni * li) / den, (ni * lr - nr * li) / den
    br, bi = b_re.astype(F32), b_im.astype(F32)
    bbar_re = q_re[..., None] * br - q_im[..., None] * bi
    bbar_im = q_re[..., None] * bi + q_im[..., None] * br
    eye = jnp.eye(gpb, dtype=F32)

    def blockdiag_in(bb):
        t = bb.reshape(nb, gpb, p, q).transpose(0, 1, 3, 2)
        return jnp.einsum('ngqp,gh->ngqhp', t, eye).reshape(nb, gpb * q, gpb * p)

    def blockdiag_out(cc):
        t = cc.reshape(nb, gpb, q, p).transpose(0, 1, 3, 2)
        return jnp.einsum('ngpq,gh->ngphq', t, eye).reshape(nb, gpb * p, gpb * q)

    b_blk = jnp.concatenate([blockdiag_in(bbar_re), blockdiag_in(bbar_im)], axis=2).astype(BF16)
    c_blk = jnp.concatenate([blockdiag_out(c_re.astype(F32)), -blockdiag_out(c_im.astype(F32))],
                            axis=1).astype(BF16)
    ar = abar_re.reshape(nb, gpb * p)
    ai = abar_im.reshape(nb, gpb * p)
    pows = [(ar, ai)]
    for _ in range(7):
        pr, pi = pows[-1]
        pows.append((pr * ar - pi * ai, pr * ai + pi * ar))
    row = jnp.arange(S5_SCAN_ROWS) % 8

    def table(vals):
        zero = jnp.zeros_like(ar)
        re = jnp.stack([vals[r][0] if vals[r] is not None else zero for r in range(8)], axis=1)
        im = jnp.stack([vals[r][1] if vals[r] is not None else zero for r in range(8)], axis=1)
        return jnp.concatenate([re[:, row], im[:, row]], axis=2)

    tabs = [table([pows[s - 1] if r >= s else None for r in range(8)]) for s in S5_LOCAL_SHIFTS]
    tabs.append(table([pows[r] for r in range(8)]))
    return b_blk, jnp.stack(tabs, axis=1), c_blk


def _s5_layer(x, norm_g, w_in, lam_re, lam_im, log_step, b_re, b_im, c_re, c_im, d_skip, w_out):
    m, d = x.shape
    (u,) = _mm(x, [(w_in.astype(BF16), 0)], lambda accs, rows, tiles: (accs[0],), [F32],
               n=d, gain=norm_g)
    b_blk, a_blk, c_blk = _s5_tables(lam_re, lam_im, log_step, b_re, b_im, c_re, c_im)
    y = _s5_core(u, b_blk, a_blk, c_blk, d_skip)
    w_out_b = w_out.astype(BF16)
    (x_new,) = _mm(y, [(w_out_b, 0), (w_out_b, d)],
                   lambda accs, rows, tiles: (tiles[0] + accs[0] * jax.nn.sigmoid(accs[1]),),
                   [F32], n=d, tiles=[x])
    return x_new


def kernel(x, norm_w, ffn_w_up, ffn_w_down, fox_w_in, fox_b_f, fox_qk_gain, fox_w_out, rwkv_mu, rwkv_w_rkv, rwkv_w0, rwkv_w1, rwkv_w2, rwkv_a0, rwkv_a1, rwkv_a2, rwkv_g1, rwkv_g2, rwkv_k_k, rwkv_k_a, rwkv_r_k, rwkv_ln_w, rwkv_ln_b, rwkv_w_out, s5_w_in, s5_lam_re, s5_lam_im, s5_log_step, s5_b_re, s5_b_im, s5_c_re, s5_c_im, s5_d, s5_w_out, final_norm):
    bsz, s, d = x.shape
    depth = norm_w.shape[0]
    outs = []
    for b in range(bsz):
        xb = x[b]
        ia = ib = ic = 0
        for i in range(depth):
            xb = _ffn(xb, norm_w[i, 0], ffn_w_up[i, 0].astype(BF16), ffn_w_down[i, 0].astype(BF16))
            mixer = i % 3
            if mixer == 0:
                xb = _fox_layer(xb, norm_w[i, 1], fox_w_in[ia], fox_b_f[ia], fox_qk_gain[ia], fox_w_out[ia])
                ia += 1
            elif mixer == 1:
                xb = _rwkv_layer(xb, norm_w[i, 1], rwkv_mu[ib], rwkv_w_rkv[ib], rwkv_w0[ib], rwkv_w1[ib],
                                 rwkv_w2[ib], rwkv_a0[ib], rwkv_a1[ib], rwkv_a2[ib], rwkv_g1[ib], rwkv_g2[ib],
                                 rwkv_k_k[ib], rwkv_k_a[ib], rwkv_r_k[ib].reshape(-1), rwkv_ln_w[ib],
                                 rwkv_ln_b[ib], rwkv_w_out[ib])
                ib += 1
            else:
                xb = _s5_layer(xb, norm_w[i, 1], s5_w_in[ic], s5_lam_re[ic], s5_lam_im[ic], s5_log_step[ic],
                               s5_b_re[ic], s5_b_im[ic], s5_c_re[ic], s5_c_im[ic], s5_d[ic], s5_w_out[ic])
                ic += 1
            fin = final_norm if i == depth - 1 else None
            xb = _ffn(xb, norm_w[i, 2], ffn_w_up[i, 1].astype(BF16), ffn_w_down[i, 1].astype(BF16), fin)
        outs.append(xb)
    return jnp.stack(outs, axis=0)
```

```python
import functools
import math

import jax
import jax.numpy as jnp
from jax import lax
from jax.experimental import pallas as pl
from jax.experimental.pallas import tpu as pltpu

F32 = jnp.float32
BF16 = jnp.bfloat16

V7X_VMEM_BYTES = 64 * 1024 * 1024
VMEM_LIMIT_BYTES = V7X_VMEM_BYTES - 8 * 1024 * 1024
LANES = 128

RMS_EPS = 1e-6
FOX_HEAD_DIM = 128
FOX_BLOCKS_PER_TRIP = 4
RWKV_HEAD_DIM = 64
RWKV_LN_EPS = 64e-5
RWKV_NORM_EPS = 1e-12
RWKV_CHUNK = 64
S5_GROUP = 16
S5_STATE = 64
S5_MAX_RE = -1e-4
S5_GROUPS_PER_BLOCK = LANES // S5_GROUP
NEG_BIG = -1e30
LOG2_E = math.log2(math.e)
S5_SCAN_ROWS = 32
S5_LOCAL_SHIFTS = (1, 2, 4)

NT_DIMS = (((1,), (1,)), ((), ()))
TN_DIMS = (((0,), (0,)), ((), ()))
NN_DIMS = (((1,), (0,)), ((), ()))


def _cparams(*sem, vmem_limit_bytes=VMEM_LIMIT_BYTES):
    return pltpu.CompilerParams(dimension_semantics=sem, vmem_limit_bytes=vmem_limit_bytes)


def _rms_scale(x):
    return lax.rsqrt(jnp.mean(x * x, axis=-1, keepdims=True) + RMS_EPS)


def _softplus(z):
    return jnp.maximum(z, 0.0) + jnp.log1p(jnp.exp(-jnp.abs(z)))


def _dot(a, b, dims=NN_DIMS):
    return lax.dot_general(a, b, dims, preferred_element_type=F32)


def _split_bf16(x, parts):
    out = []
    rem = x
    for _ in range(parts):
        p = rem.astype(BF16)
        out.append(p)
        rem = rem - p.astype(F32)
    return out


def _dot_f32(a, b, dims=NN_DIMS, passes=3):
    if passes == 1:
        return _dot(a.astype(BF16), b.astype(BF16), dims)
    a_p = _split_bf16(a, 2 if passes == 3 else 3)
    b_p = _split_bf16(b, 2 if passes == 3 else 3)
    acc = None
    for ia, ap in enumerate(a_p):
        for ib, bp in enumerate(b_p):
            if ia + ib >= len(a_p):
                continue
            t = _dot(ap, bp, dims)
            acc = t if acc is None else acc + t
    return acc


def _ffn_kernel(x_ref, g_ref, wg_ref, wu_ref, wd_ref, *rest, final_norm):
    if final_norm:
        fin_ref, o_ref, h_ref = rest
    else:
        o_ref, h_ref = rest
    j = pl.program_id(1)

    @pl.when(j == 0)
    def _():
        x = x_ref[...]
        h_ref[...] = (x * _rms_scale(x) * g_ref[...]).astype(BF16)
        o_ref[...] = jnp.zeros_like(o_ref)

    h = h_ref[...]
    gate = _dot(h, wg_ref[...])
    up = _dot(h, wu_ref[...])
    act = (gate * jax.nn.sigmoid(gate) * up).astype(BF16)
    o_ref[...] += _dot(act, wd_ref[...])

    @pl.when(j == pl.num_programs(1) - 1)
    def _():
        y = x_ref[...] + 0.5 * o_ref[...]
        if final_norm:
            y = y * _rms_scale(y) * fin_ref[...]
        o_ref[...] = y


def _ffn(x, g, w_up, w_down, fin=None, *, tm=512, tf=512):
    m, d = x.shape
    f = w_down.shape[0]
    tm = min(tm, m)
    nf = f // tf
    in_specs = [
        pl.BlockSpec((tm, d), lambda i, j: (i, 0)),
        pl.BlockSpec((1, d), lambda i, j: (0, 0)),
        pl.BlockSpec((d, tf), lambda i, j: (0, j)),
        pl.BlockSpec((d, tf), lambda i, j: (0, j + nf)),
        pl.BlockSpec((tf, d), lambda i, j: (j, 0)),
    ]
    args = [x, g.reshape(1, d), w_up, w_up, w_down]
    if fin is not None:
        in_specs.append(pl.BlockSpec((1, d), lambda i, j: (0, 0)))
        args.append(fin.reshape(1, d))
    return pl.pallas_call(
        functools.partial(_ffn_kernel, final_norm=fin is not None),
        grid=(m // tm, nf),
        in_specs=in_specs,
        out_specs=pl.BlockSpec((tm, d), lambda i, j: (i, 0)),
        out_shape=jax.ShapeDtypeStruct((m, d), F32),
        scratch_shapes=[pltpu.VMEM((tm, d), BF16)],
        compiler_params=_cparams("parallel", "arbitrary"),
        name="ffn",
    )(*args)


def _mm_kernel(*refs, n_b, n_row, n_tile, n_out, rmsnorm, epilogue):
    pos = 0
    a_ref = refs[pos]; pos += 1
    if rmsnorm:
        g_ref = refs[pos]; pos += 1
    b_refs = refs[pos:pos + n_b]; pos += n_b
    row_refs = refs[pos:pos + n_row]; pos += n_row
    tile_refs = refs[pos:pos + n_tile]; pos += n_tile
    out_refs = refs[pos:pos + n_out]; pos += n_out
    if rmsnorm:
        h_ref = refs[pos]

        @pl.when(pl.program_id(1) == 0)
        def _():
            x = a_ref[...]
            h_ref[...] = (x * _rms_scale(x) * g_ref[...]).astype(BF16)

        a = h_ref[...]
    else:
        a = a_ref[...]
    accs = [_dot(a, b_ref[...]) for b_ref in b_refs]
    outs = epilogue(accs, [r[...] for r in row_refs], [t[...] for t in tile_refs])
    for o_ref, o in zip(out_refs, outs):
        o_ref[...] = o.astype(o_ref.dtype)


def _mm(a, bs, epilogue, out_dtypes, *, n, rows=(), tiles=(), gain=None, tm=512, tn=None):
    m, k = a.shape
    tm = min(tm, m)
    if tn is None:
        tn = max(n // len(bs), 4 * LANES)
    tn = min(tn, n)
    rmsnorm = gain is not None
    in_specs = [pl.BlockSpec((tm, k), lambda i, j: (i, 0))]
    args = [a]
    if rmsnorm:
        in_specs.append(pl.BlockSpec((1, k), lambda i, j: (0, 0)))
        args.append(gain.reshape(1, k))
    for b, off in bs:
        ob = off // tn
        in_specs.append(pl.BlockSpec((k, tn), lambda i, j, ob=ob: (0, j + ob)))
        args.append(b)
    for r in rows:
        if r.shape[1] == tn and n != tn:
            in_specs.append(pl.BlockSpec((1, tn), lambda i, j: (0, 0)))
        else:
            in_specs.append(pl.BlockSpec((1, tn), lambda i, j: (0, j)))
        args.append(r)
    for t in tiles:
        in_specs.append(pl.BlockSpec((tm, tn), lambda i, j: (i, j)))
        args.append(t)
    out_shape = [jax.ShapeDtypeStruct((m, n), dt) for dt in out_dtypes]
    out_specs = [pl.BlockSpec((tm, tn), lambda i, j: (i, j)) for _ in out_dtypes]
    kern = functools.partial(
        _mm_kernel, n_b=len(bs), n_row=len(rows), n_tile=len(tiles), n_out=len(out_dtypes),
        rmsnorm=rmsnorm, epilogue=epilogue)
    return pl.pallas_call(
        kern,
        grid=(m // tm, n // tn),
        in_specs=in_specs,
        out_specs=out_specs,
        out_shape=out_shape,
        scratch_shapes=[pltpu.VMEM((tm, k), BF16)] if rmsnorm else [],
        compiler_params=_cparams("parallel", "arbitrary"),
        name="proj",
    )(*args)


def _fox_qkvg_epilogue(accs, rows, tiles):
    gq, gk = rows
    q, k, v, g = accs
    tn = q.shape[1]

    def head_norm(t, gain):
        parts = []
        for hh in range(tn // FOX_HEAD_DIM):
            sl = t[:, hh * FOX_HEAD_DIM:(hh + 1) * FOX_HEAD_DIM]
            parts.append(sl * _rms_scale(sl))
        return jnp.concatenate(parts, axis=1) * gain

    return head_norm(q, gq), head_norm(k, gk), v, jax.nn.sigmoid(g)


def _fox_forget_kernel(x_ref, g_ref, wf_ref, bf_ref, tri_ref, c_ref, carry_ref):
    i = pl.program_id(0)

    @pl.when(i == 0)
    def _():
        carry_ref[...] = jnp.zeros_like(carry_ref)

    x = x_ref[...]
    h = (x * _rms_scale(x) * g_ref[...]).astype(BF16)
    f_logit = _dot(wf_ref[...], h, NT_DIMS) + bf_ref[...]
    log_f = -_softplus(-f_logit) * LOG2_E
    tri = tri_ref[...]
    local = None
    for part in _split_bf16(log_f, 3):
        t = _dot(part, tri)
        local = t if local is None else local + t
    c = local + carry_ref[...]
    c_ref[...] = c
    carry_ref[...] = c[:, -1:]


def _fox_forget(x, g, wf_t, b_f, *, tm=512):
    m, d = x.shape
    nh = wf_t.shape[0]
    tm = min(tm, m)
    tri = jnp.triu(jnp.ones((tm, tm), F32)).astype(BF16)
    return pl.pallas_call(
        _fox_forget_kernel,
        grid=(m // tm,),
        in_specs=[
            pl.BlockSpec((tm, d), lambda i: (i, 0)),
            pl.BlockSpec((1, d), lambda i: (0, 0)),
            pl.BlockSpec((nh, d), lambda i: (0, 0)),
            pl.BlockSpec((nh, 1), lambda i: (0, 0)),
            pl.BlockSpec((tm, tm), lambda i: (0, 0)),
        ],
        out_specs=pl.BlockSpec((nh, tm), lambda i: (0, i)),
        out_shape=jax.ShapeDtypeStruct((nh, m), F32),
        scratch_shapes=[pltpu.VMEM((nh, 1), F32)],
        compiler_params=_cparams("arbitrary"),
        name="fox_forget",
    )(x, g.reshape(1, d), wf_t, b_f.reshape(nh, 1), tri)


def _fox_attn_kernel(q_ref, k_ref, v_ref, cq_ref, ck_ref, g_ref, o_ref, m_ref, acc_ref,
                     *, tq, tk, rb):
    i = pl.program_id(1)
    n_rb = tq // rb
    kpq = tq // tk
    m_ref[...] = jnp.full_like(m_ref, NEG_BIG)
    acc_ref[...] = jnp.zeros_like(acc_ref)
    c_first = cq_ref[0, 0][:, 0:1]
    dh = q_ref.shape[1]

    def step(j, row_blocks):
        ks = pl.multiple_of(j * tk, tk)
        k = k_ref[pl.ds(ks, tk), :]
        v = v_ref[pl.ds(ks, tk), :]
        v_aug = jnp.concatenate([v, jnp.ones_like(v)], axis=1)
        bias = c_first - ck_ref[0, j]
        for r, mask in row_blocks:
            rows = pl.ds(r * rb, rb)
            s = _dot(q_ref[rows, :], k, NT_DIMS) + bias
            if mask is not None:
                s = jnp.where(mask, s, NEG_BIG)
            m_prev = m_ref[rows, :]
            m_new = jnp.maximum(m_prev, jnp.max(s, axis=-1, keepdims=True))
            alpha = jnp.exp2(m_prev - m_new)
            pmat = jnp.exp2(s - jnp.concatenate([m_new] * (tk // dh), axis=1))
            acc_ref[rows, :] = (jnp.concatenate([alpha, alpha], axis=1) * acc_ref[rows, :]
                                + _dot(pmat.astype(BF16), v_aug))
            m_ref[rows, :] = m_new

    full = [(r, None) for r in range(n_rb)]

    def make_body(n_blocks, first):
        def body(jo, carry):
            for jj in range(n_blocks):
                step(first + jo * n_blocks + jj, full)
            return carry
        return body

    n_full = i * kpq
    n_big = n_full // FOX_BLOCKS_PER_TRIP
    lax.fori_loop(0, n_big, make_body(FOX_BLOCKS_PER_TRIP, 0), 0)
    n_done = n_big * FOX_BLOCKS_PER_TRIP
    lax.fori_loop(0, (n_full - n_done) // kpq, make_body(kpq, n_done), 0)
    for jj in range(kpq):
        blocks = []
        for r in range(n_rb):
            r0, r1 = r * rb, (r + 1) * rb - 1
            c0, c1 = jj * tk, (jj + 1) * tk - 1
            if r1 < c0:
                continue
            if r0 >= c1:
                blocks.append((r, None))
            else:
                row = r0 + lax.broadcasted_iota(jnp.int32, (rb, tk), 0)
                col = c0 + lax.broadcasted_iota(jnp.int32, (rb, tk), 1)
                blocks.append((r, col <= row))
        step(i * kpq + jj, blocks)
    o = acc_ref[:, :dh] / acc_ref[:, dh:]
    o_ref[...] = (o * g_ref[...].astype(F32)).astype(o_ref.dtype)


def _fox_attention(q, k, v, c, gate, *, tq=1024, tk=512, rb=256):
    m, d = q.shape
    nh = d // FOX_HEAD_DIM
    tq = min(tq, m)
    tk = min(tk, tq)
    rb = min(rb, tq)
    dh = FOX_HEAD_DIM
    cq = c.reshape(nh, m // tq, 1, tq)
    ck = c.reshape(nh, m // tk, 1, tk)
    return pl.pallas_call(
        functools.partial(_fox_attn_kernel, tq=tq, tk=tk, rb=rb),
        grid=(nh, m // tq),
        in_specs=[
            pl.BlockSpec((tq, dh), lambda h, i: (i, h)),
            pl.BlockSpec((m, dh), lambda h, i: (0, h)),
            pl.BlockSpec((m, dh), lambda h, i: (0, h)),
            pl.BlockSpec((1, 1, 1, tq), lambda h, i: (h, i, 0, 0)),
            pl.BlockSpec((1, m // tk, 1, tk), lambda h, i: (h, 0, 0, 0)),
            pl.BlockSpec((tq, dh), lambda h, i: (i, h)),
        ],
        out_specs=pl.BlockSpec((tq, dh), lambda h, i: (i, h)),
        out_shape=jax.ShapeDtypeStruct((m, d), BF16),
        scratch_shapes=[
            pltpu.VMEM((tq, dh), F32),
            pltpu.VMEM((tq, 2 * dh), F32),
        ],
        compiler_params=_cparams("parallel", "arbitrary"),
        name="fox_attn",
    )(q, k, v, cq, ck, gate)


def _fox_layer(x, norm_g, w_in, b_f, qk_gain, w_out):
    m, d = x.shape
    nh = d // FOX_HEAD_DIM
    w_in_b = w_in.astype(BF16)
    tn = 512
    scale = FOX_HEAD_DIM ** -0.5 * LOG2_E
    gq = jnp.tile(qk_gain[0] * scale, tn // FOX_HEAD_DIM).reshape(1, tn)
    gk = jnp.tile(qk_gain[1], tn // FOX_HEAD_DIM).reshape(1, tn)
    q, k, v, gate = _mm(
        x, [(w_in_b, 0), (w_in_b, d), (w_in_b, 2 * d), (w_in_b, 3 * d)],
        _fox_qkvg_epilogue, [BF16, BF16, BF16, BF16], n=d, rows=[gq, gk], gain=norm_g, tn=tn)
    wf_t = w_in[:, 4 * d:].T.astype(BF16)
    c = _fox_forget(x, norm_g, wf_t, b_f)
    og = _fox_attention(q, k, v, c, gate)
    (x_new,) = _mm(og, [(w_out.astype(BF16), 0)],
                   lambda accs, rows, tiles: (tiles[0] + accs[0],), [F32], n=d, tiles=[x])
    return x_new


def _rwkv_prep_kernel(x_ref, xp_ref, g_ref, mu_ref, *rest):
    out_refs = rest[:6]
    hbuf = rest[6]
    i = pl.program_id(0)
    tm = x_ref.shape[0]
    x = x_ref[...]
    g = g_ref[...]
    h = x * _rms_scale(x) * g
    xp = xp_ref[...][7:8, :]
    hp = xp * _rms_scale(xp) * g
    hp = jnp.where(i == 0, jnp.zeros_like(hp), hp)
    hbuf[pl.ds(8, tm), :] = h
    hbuf[pl.ds(7, 1), :] = hp
    xx = hbuf[pl.ds(7, tm), :] - h
    mu = mu_ref[...]
    for n, o_ref in enumerate(out_refs):
        o_ref[...] = (h + xx * mu[n:n + 1, :]).astype(o_ref.dtype)


def _rwkv_prep(x, g, mu, *, tm=256):
    m, d = x.shape
    tm = min(tm, m)
    rb = tm // 8
    return pl.pallas_call(
        _rwkv_prep_kernel,
        grid=(m // tm,),
        in_specs=[
            pl.BlockSpec((tm, d), lambda i: (i, 0)),
            pl.BlockSpec((8, d), lambda i: (jnp.maximum(i * rb - 1, 0), 0)),
            pl.BlockSpec((1, d), lambda i: (0, 0)),
            pl.BlockSpec((8, d), lambda i: (0, 0)),
        ],
        out_specs=[pl.BlockSpec((tm, d), lambda i: (i, 0)) for _ in range(6)],
        out_shape=[jax.ShapeDtypeStruct((m, d), BF16) for _ in range(6)],
        scratch_shapes=[pltpu.VMEM((tm + 8, d), F32)],
        compiler_params=_cparams("parallel"),
        name="rwkv_prep",
    )(x, x, g.reshape(1, d), jnp.pad(mu, ((0, 2), (0, 0))))


def _lora_kernel(x_ref, w1_ref, w2_ref, b_ref, o_ref, *, mid_act, out_act):
    t = _dot(x_ref[...], w1_ref[...])
    t = mid_act(t).astype(BF16)
    y = _dot(t, w2_ref[...]) + b_ref[...]
    o_ref[...] = out_act(y).astype(o_ref.dtype)


def _lora(x, w1, w2, bias, mid_act, out_act, *, tm=512):
    m, d = x.shape
    r = w1.shape[1]
    n = w2.shape[1]
    tm = min(tm, m)
    return pl.pallas_call(
        functools.partial(_lora_kernel, mid_act=mid_act, out_act=out_act),
        grid=(m // tm,),
        in_specs=[
            pl.BlockSpec((tm, d), lambda i: (i, 0)),
            pl.BlockSpec((d, r), lambda i: (0, 0)),
            pl.BlockSpec((r, n), lambda i: (0, 0)),
            pl.BlockSpec((1, n), lambda i: (0, 0)),
        ],
        out_specs=pl.BlockSpec((tm, n), lambda i: (i, 0)),
        out_shape=jax.ShapeDtypeStruct((m, n), F32),
        compiler_params=_cparams("parallel"),
        name="rwkv_lora",
    )(x, w1.astype(BF16), w2.astype(BF16), bias.reshape(1, n))


def _tri_mask(n, strict):
    row = lax.broadcasted_iota(jnp.int32, (n, n), 0)
    col = lax.broadcasted_iota(jnp.int32, (n, n), 1)
    return (col < row) if strict else (col <= row)


def _bdot(a, b, spec, passes=3):
    ein = lambda x, y: jnp.einsum(spec, x, y, preferred_element_type=F32)
    if passes == 1:
        return ein(a.astype(BF16), b.astype(BF16))
    a_hi, a_lo = _split_bf16(a, 2)
    b_hi, b_lo = _split_bf16(b, 2)
    return ein(a_hi, b_hi) + ein(a_hi, b_lo) + ein(a_lo, b_hi)


def _rwkv_core_kernel(r_ref, k_ref, v_ref, ld_ref, a_ref, g_ref, kk_ref, ka_ref, rk_ref,
                      lnw_ref, lnb_ref, tri_ref, o_ref, st_ref, *, chunk):
    t_idx = pl.program_id(1)

    @pl.when(t_idx == 0)
    def _():
        st_ref[...] = jnp.zeros_like(st_ref)

    tt = r_ref.shape[0]
    c = chunk
    nc = tt // c
    n = RWKV_HEAD_DIM
    shape3 = (nc, c, LANES)
    row_l = lax.broadcasted_iota(jnp.int32, (LANES, LANES), 0)
    col_l = lax.broadcasted_iota(jnp.int32, (LANES, LANES), 1)
    same_head = (row_l < n) == (col_l < n)
    seg_ones = same_head.astype(BF16)
    bd_mask = same_head.astype(F32)
    eye_l = (row_l == col_l).astype(F32)

    def seg_sum(x):
        acc = None
        for part in _split_bf16(x.reshape(tt, LANES), 2):
            t = _dot(part, seg_ones)
            acc = t if acc is None else acc + t
        return acc.reshape(shape3)

    r = r_ref[...].reshape(shape3)
    k = k_ref[...].reshape(shape3)
    v = v_ref[...].reshape(shape3)
    ld = ld_ref[...]
    a = a_ref[...].reshape(shape3)

    lane = lax.broadcasted_iota(jnp.int32, (1, 1, LANES), 2)
    head0 = lane < n
    m0 = head0.astype(F32)
    m1 = 1.0 - m0
    both = lambda x: jnp.concatenate([x, x], axis=0)
    own = lambda x: jnp.where(head0, x[:nc], x[nc:])

    kk = k * kk_ref[...]
    kk = kk / jnp.maximum(jnp.sqrt(seg_sum(kk * kk)), RWKV_NORM_EPS)
    k2 = k * (1.0 + (a - 1.0) * ka_ref[...])
    av = -kk
    bv = kk * a
    gam = None
    ld = ld.reshape(shape3)
    tri = jnp.broadcast_to(tri_ref[...][None], (nc, c, c))
    for part in _split_bf16(ld, 3):
        t = jnp.einsum('bts,bsl->btl', tri, part, preferred_element_type=F32)
        gam = t if gam is None else gam + t
    g_end = gam[:, c - 1:c, :]
    at = av * jnp.exp(gam - ld)
    rt = r * jnp.exp(gam)
    e_neg = jnp.exp(-gam)
    bt = bv * e_neg
    kt = k2 * e_neg
    e_end = jnp.exp(g_end - gam)
    b_end = bv * e_end
    k_end = k2 * e_end

    lhs = both(jnp.concatenate([at, rt], axis=1))
    gb = _bdot(lhs, jnp.concatenate([bt * m0, bt * m1], axis=0), 'bqd,bkd->bqk')
    gk = _bdot(lhs, jnp.concatenate([kt * m0, kt * m1], axis=0), 'bqd,bkd->bqk')
    strict = _tri_mask(c, True)[None]
    incl = _tri_mask(c, False)[None]
    g_ab = jnp.where(strict, gb[:, :c], 0.0)
    g_rb = jnp.where(incl, gb[:, c:], 0.0)
    g_ak = jnp.where(strict, gk[:, :c], 0.0)
    g_rk = jnp.where(incl, gk[:, c:], 0.0)
    eye_c = (lax.broadcasted_iota(jnp.int32, (c, c), 0)
             == lax.broadcasted_iota(jnp.int32, (c, c), 1)).astype(F32)[None]
    tinv = eye_c + g_ab
    npow = g_ab
    for _ in range(int(math.log2(c)) - 1):
        npow = _bdot(npow, npow, 'bij,bjk->bik', passes=1)
        tinv = tinv + _bdot(tinv, npow, 'bij,bjk->bik', passes=1)
    apply = lambda w, x: _bdot(w, x, 'bts,bsl->btl', passes=1)
    outer = lambda x, y: _bdot(x, y, 'bti,btj->bij', passes=1)
    v2 = both(v)
    av_v = own(apply(g_ak, v2))
    pmat = own(apply(tinv, both(at)))
    qmat = own(apply(tinv, both(av_v)))
    m_mat = outer(b_end, pmat) * bd_mask + eye_l * jnp.exp(g_end)
    n_mat = (outer(b_end, qmat) + outer(k_end, v)) * bd_mask
    o1 = rt + own(apply(g_rb, both(pmat)))
    o2 = own(apply(g_rb, both(qmat)) + apply(g_rk, v2))

    st = st_ref[...]
    outs = []
    for cc in range(nc):
        outs.append(_dot_f32(o1[cc], st, passes=1) + o2[cc])
        st = _dot_f32(m_mat[cc], st, passes=1) + n_mat[cc]
    st_ref[...] = st
    o = jnp.stack(outs, axis=0)

    inv_n = 1.0 / n
    mean = seg_sum(o) * inv_n
    cen = o - mean
    var = seg_sum(cen * cen) * inv_n
    y = cen * lax.rsqrt(var + RWKV_LN_EPS) * lnw_ref[...] + lnb_ref[...]
    bonus = seg_sum(r * k2 * rk_ref[...]) * v
    out = (y + bonus) * g_ref[...].reshape(shape3)
    o_ref[...] = out.reshape(tt, LANES).astype(o_ref.dtype)


def _rwkv_core(r, k, v, ld, a, g, k_k, k_a, r_k, ln_w, ln_b, *, tt=1024):
    m, d = r.shape
    tt = min(tt, m)
    chunk = min(RWKV_CHUNK, tt)
    tri = jnp.tril(jnp.ones((chunk, chunk), F32)).astype(BF16)
    seq = pl.BlockSpec((tt, LANES), lambda hp, t: (t, hp))
    par = pl.BlockSpec((1, LANES), lambda hp, t: (0, hp))
    return pl.pallas_call(
        functools.partial(_rwkv_core_kernel, chunk=chunk),
        grid=(d // LANES, m // tt),
        in_specs=[seq] * 6 + [par] * 5 + [pl.BlockSpec((chunk, chunk), lambda hp, t: (0, 0))],
        out_specs=seq,
        out_shape=jax.ShapeDtypeStruct((m, d), BF16),
        scratch_shapes=[pltpu.VMEM((LANES, LANES), F32)],
        compiler_params=_cparams("parallel", "arbitrary"),
        name="rwkv_core",
    )(r, k, v, ld, a, g, k_k.reshape(1, d), k_a.reshape(1, d), r_k.reshape(1, d),
      ln_w.reshape(1, d), ln_b.reshape(1, d), tri)


def _rwkv_layer(x, norm_g, mu, w_rkv, w0, w1, w2, a0, a1, a2, g1, g2, k_k, k_a, r_k, ln_w, ln_b, w_out):
    m, d = x.shape
    xr, xw, xk, xv, xa, xg = _rwkv_prep(x, norm_g, mu)
    w_rkv_b = w_rkv.astype(BF16)
    plain = lambda accs, rows, tiles: (accs[0],)
    (r,) = _mm(xr, [(w_rkv_b[0], 0)], plain, [F32], n=d)
    (k,) = _mm(xk, [(w_rkv_b[1], 0)], plain, [F32], n=d)
    (v,) = _mm(xv, [(w_rkv_b[2], 0)], plain, [F32], n=d)
    ident = lambda t: t
    ld = _lora(xw, w1, w2, w0, jnp.tanh, lambda y: -jnp.exp(-_softplus(-y) - 0.5))
    a = _lora(xa, a1, a2, a0, ident, jax.nn.sigmoid)
    g = _lora(xg, g1, g2, jnp.zeros((d,), F32), jax.nn.sigmoid, ident)
    y = _rwkv_core(r, k, v, ld, a, g, k_k, k_a, r_k, ln_w, ln_b)
    (x_new,) = _mm(y, [(w_out.astype(BF16), 0)],
                   lambda accs, rows, tiles: (tiles[0] + accs[0],), [F32], n=d, tiles=[x])
    return x_new


def _s5_core_kernel(u_ref, b_ref, tab_ref, c_ref, d_ref, o_ref, hbuf, cbuf, hb16, carry_ref):
    t_idx = pl.program_id(1)
    tl = u_ref.shape[0]
    width = hbuf.shape[1]
    half = width // 2
    rc = tab_ref.shape[2]
    n_lvl = len(S5_LOCAL_SHIFTS)
    re = pl.ds(0, half)
    im = pl.ds(half, half)

    @pl.when(t_idx == 0)
    def _():
        carry_ref[...] = jnp.zeros_like(carry_ref)

    u = u_ref[...]
    hbuf[...] = _dot(u.astype(BF16), b_ref[0])

    def local(n, carry):
        rows = pl.ds(pl.multiple_of(n * rc, rc), rc)
        xr = hbuf[rows, re]
        xi = hbuf[rows, im]
        for lvl, s in enumerate(S5_LOCAL_SHIFTS):
            tr = tab_ref[0, lvl, :, re]
            ti = tab_ref[0, lvl, :, im]
            sr = pltpu.roll(xr, s, axis=0)
            si = pltpu.roll(xi, s, axis=0)
            xr, xi = xr + tr * sr - ti * si, xi + tr * si + ti * sr
        hbuf[rows, re] = xr
        hbuf[rows, im] = xi
        return carry

    lax.fori_loop(0, tl // rc, local, 0)

    a8r = tab_ref[0, n_lvl, pl.ds(7, 1), re]
    a8i = tab_ref[0, n_lvl, pl.ds(7, 1), im]
    cr = carry_ref[0:1, re]
    ci = carry_ref[0:1, im]
    for g in range(tl // 8):
        cbuf[pl.ds(8 * g, 8), re] = jnp.broadcast_to(cr, (8, half))
        cbuf[pl.ds(8 * g, 8), im] = jnp.broadcast_to(ci, (8, half))
        lr = hbuf[pl.ds(8 * g + 7, 1), re]
        li = hbuf[pl.ds(8 * g + 7, 1), im]
        cr, ci = lr + a8r * cr - a8i * ci, li + a8r * ci + a8i * cr
    carry_ref[0:1, re] = cr
    carry_ref[0:1, im] = ci

    def apply(n, carry):
        rows = pl.ds(pl.multiple_of(n * rc, rc), rc)
        pr = tab_ref[0, n_lvl, :, re]
        pi = tab_ref[0, n_lvl, :, im]
        br = cbuf[rows, re]
        bi = cbuf[rows, im]
        hb16[rows, re] = (hbuf[rows, re] + pr * br - pi * bi).astype(BF16)
        hb16[rows, im] = (hbuf[rows, im] + pr * bi + pi * br).astype(BF16)
        return carry

    lax.fori_loop(0, tl // rc, apply, 0)
    y = _dot(hb16[...], c_ref[0]) + d_ref[...] * u
    o_ref[...] = jax.nn.gelu(y).astype(o_ref.dtype)


def _s5_core(u, b_blk, tab, c_blk, d_skip, *, tl=512):
    m, d = u.shape
    tl = min(tl, m)
    nb = d // LANES
    width = b_blk.shape[2]
    n_tab, rc = tab.shape[1], tab.shape[2]
    return pl.pallas_call(
        _s5_core_kernel,
        grid=(nb, m // tl),
        in_specs=[
            pl.BlockSpec((tl, LANES), lambda gb, t: (t, gb)),
            pl.BlockSpec((1, LANES, width), lambda gb, t: (gb, 0, 0)),
            pl.BlockSpec((1, n_tab, rc, width), lambda gb, t: (gb, 0, 0, 0)),
            pl.BlockSpec((1, width, LANES), lambda gb, t: (gb, 0, 0)),
            pl.BlockSpec((1, LANES), lambda gb, t: (0, gb)),
        ],
        out_specs=pl.BlockSpec((tl, LANES), lambda gb, t: (t, gb)),
        out_shape=jax.ShapeDtypeStruct((m, d), BF16),
        scratch_shapes=[pltpu.VMEM((tl, width), F32), pltpu.VMEM((tl, width), F32),
                        pltpu.VMEM((tl, width), BF16), pltpu.VMEM((8, width), F32)],
        compiler_params=_cparams("parallel", "arbitrary"),
        name="s5_core",
    )(u, b_blk, tab, c_blk, d_skip.reshape(1, d))


def _s5_tables(lam_re, lam_im, log_step, b_re, b_im, c_re, c_im):
    g, p = lam_re.shape
    q = b_re.shape[2]
    gpb = S5_GROUPS_PER_BLOCK
    nb = g // gpb
    lr = jnp.minimum(lam_re.astype(F32), S5_MAX_RE)
    li = lam_im.astype(F32)
    dt = jnp.exp(log_step.astype(F32))[:, None]
    mag = jnp.exp(lr * dt)
    abar_re, abar_im = mag * jnp.cos(li * dt), mag * jnp.sin(li * dt)
    den = lr * lr + li * li
    nr, ni = abar_re - 1.0, abar_im
    q_re, q_im = (nr * lr + ni * li) / den, (ni * lr - nr * li) / den
    br, bi = b_re.astype(F32), b_im.astype(F32)
    bbar_re = q_re[..., None] * br - q_im[..., None] * bi
    bbar_im = q_re[..., None] * bi + q_im[..., None] * br
    eye = jnp.eye(gpb, dtype=F32)

    def blockdiag_in(bb):
        t = bb.reshape(nb, gpb, p, q).transpose(0, 1, 3, 2)
        return jnp.einsum('ngqp,gh->ngqhp', t, eye).reshape(nb, gpb * q, gpb * p)

    def blockdiag_out(cc):
        t = cc.reshape(nb, gpb, q, p).transpose(0, 1, 3, 2)
        return jnp.einsum('ngpq,gh->ngphq', t, eye).reshape(nb, gpb * p, gpb * q)

    b_blk = jnp.concatenate([blockdiag_in(bbar_re), blockdiag_in(bbar_im)], axis=2).astype(BF16)
    c_blk = jnp.concatenate([blockdiag_out(c_re.astype(F32)), -blockdiag_out(c_im.astype(F32))],
                            axis=1).astype(BF16)
    ar = abar_re.reshape(nb, gpb * p)
    ai = abar_im.reshape(nb, gpb * p)
    pows = [(ar, ai)]
    for _ in range(7):
        pr, pi = pows[-1]
        pows.append((pr * ar - pi * ai, pr * ai + pi * ar))
    row = jnp.arange(S5_SCAN_ROWS) % 8

    def table(vals):
        zero = jnp.zeros_like(ar)
        re = jnp.stack([vals[r][0] if vals[r] is not None else zero for r in range(8)], axis=1)
        im = jnp.stack([vals[r][1] if vals[r] is not None else zero for r in range(8)], axis=1)
        return jnp.concatenate([re[:, row], im[:, row]], axis=2)

    tabs = [table([pows[s - 1] if r >= s else None for r in range(8)]) for s in S5_LOCAL_SHIFTS]
    tabs.append(table([pows[r] for r in range(8)]))
    return b_blk, jnp.stack(tabs, axis=1), c_blk


def _s5_layer(x, norm_g, w_in, lam_re, lam_im, log_step, b_re, b_im, c_re, c_im, d_skip, w_out):
    m, d = x.shape
    (u,) = _mm(x, [(w_in.astype(BF16), 0)], lambda accs, rows, tiles: (accs[0],), [F32],
               n=d, gain=norm_g)
    b_blk, a_blk, c_blk = _s5_tables(lam_re, lam_im, log_step, b_re, b_im, c_re, c_im)
    y = _s5_core(u, b_blk, a_blk, c_blk, d_skip)
    w_out_b = w_out.astype(BF16)
    (x_new,) = _mm(y, [(w_out_b, 0), (w_out_b, d)],
                   lambda accs, rows, tiles: (tiles[0] + accs[0] * jax.nn.sigmoid(accs[1]),),
                   [F32], n=d, tiles=[x])
    return x_new


def kernel(x, norm_w, ffn_w_up, ffn_w_down, fox_w_in, fox_b_f, fox_qk_gain, fox_w_out, rwkv_mu, rwkv_w_rkv, rwkv_w0, rwkv_w1, rwkv_w2, rwkv_a0, rwkv_a1, rwkv_a2, rwkv_g1, rwkv_g2, rwkv_k_k, rwkv_k_a, rwkv_r_k, rwkv_ln_w, rwkv_ln_b, rwkv_w_out, s5_w_in, s5_lam_re, s5_lam_im, s5_log_step, s5_b_re, s5_b_im, s5_c_re, s5_c_im, s5_d, s5_w_out, final_norm):
    bsz, s, d = x.shape
    depth = norm_w.shape[0]
    outs = []
    for b in range(bsz):
        xb = x[b]
        ia = ib = ic = 0
        for i in range(depth):
            xb = _ffn(xb, norm_w[i, 0], ffn_w_up[i, 0].astype(BF16), ffn_w_down[i, 0].astype(BF16))
            mixer = i % 3
            if mixer == 0:
                xb = _fox_layer(xb, norm_w[i, 1], fox_w_in[ia], fox_b_f[ia], fox_qk_gain[ia], fox_w_out[ia])
                ia += 1
            elif mixer == 1:
                xb = _rwkv_layer(xb, norm_w[i, 1], rwkv_mu[ib], rwkv_w_rkv[ib], rwkv_w0[ib], rwkv_w1[ib],
                                 rwkv_w2[ib], rwkv_a0[ib], rwkv_a1[ib], rwkv_a2[ib], rwkv_g1[ib], rwkv_g2[ib],
                                 rwkv_k_k[ib], rwkv_k_a[ib], rwkv_r_k[ib].reshape(-1), rwkv_ln_w[ib],
                                 rwkv_ln_b[ib], rwkv_w_out[ib])
                ib += 1
            else:
                xb = _s5_layer(xb, norm_w[i, 1], s5_w_in[ic], s5_lam_re[ic], s5_lam_im[ic], s5_log_step[ic],
                               s5_b_re[ic], s5_b_im[ic], s5_c_re[ic], s5_c_im[ic], s5_d[ic], s5_w_out[ic])
                ic += 1
            fin = final_norm if i == depth - 1 else None
            xb = _ffn(xb, norm_w[i, 2], ffn_w_up[i, 1].astype(BF16), ffn_w_down[i, 1].astype(BF16), fin)
        outs.append(xb)
    return jnp.stack(outs, axis=0)
```

```python
import functools
import math

import jax
import jax.numpy as jnp
from jax import lax
from jax.experimental import pallas as pl
from jax.experimental.pallas import tpu as pltpu

F32 = jnp.float32
BF16 = jnp.bfloat16

V7X_VMEM_BYTES = 64 * 1024 * 1024
VMEM_LIMIT_BYTES = V7X_VMEM_BYTES - 8 * 1024 * 1024
LANES = 128

RMS_EPS = 1e-6
FOX_HEAD_DIM = 128
FOX_BLOCKS_PER_TRIP = 4
RWKV_HEAD_DIM = 64
RWKV_LN_EPS = 64e-5
RWKV_NORM_EPS = 1e-12
RWKV_CHUNK = 64
S5_GROUP = 16
S5_STATE = 64
S5_MAX_RE = -1e-4
S5_GROUPS_PER_BLOCK = LANES // S5_GROUP
NEG_BIG = -1e30
LOG2_E = math.log2(math.e)
S5_SCAN_ROWS = 32
S5_LOCAL_SHIFTS = (1, 2, 4)

NT_DIMS = (((1,), (1,)), ((), ()))
TN_DIMS = (((0,), (0,)), ((), ()))
NN_DIMS = (((1,), (0,)), ((), ()))


def _cparams(*sem, vmem_limit_bytes=VMEM_LIMIT_BYTES):
    return pltpu.CompilerParams(dimension_semantics=sem, vmem_limit_bytes=vmem_limit_bytes)


def _rms_scale(x):
    return lax.rsqrt(jnp.mean(x * x, axis=-1, keepdims=True) + RMS_EPS)


def _softplus(z):
    return jnp.maximum(z, 0.0) + jnp.log1p(jnp.exp(-jnp.abs(z)))


def _dot(a, b, dims=NN_DIMS):
    return lax.dot_general(a, b, dims, preferred_element_type=F32)


def _split_bf16(x, parts):
    out = []
    rem = x
    for _ in range(parts):
        p = rem.astype(BF16)
        out.append(p)
        rem = rem - p.astype(F32)
    return out


def _dot_f32(a, b, dims=NN_DIMS, passes=3):
    if passes == 1:
        return _dot(a.astype(BF16), b.astype(BF16), dims)
    a_p = _split_bf16(a, 2 if passes == 3 else 3)
    b_p = _split_bf16(b, 2 if passes == 3 else 3)
    acc = None
    for ia, ap in enumerate(a_p):
        for ib, bp in enumerate(b_p):
            if ia + ib >= len(a_p):
                continue
            t = _dot(ap, bp, dims)
            acc = t if acc is None else acc + t
    return acc


def _ffn_kernel(x_ref, g_ref, wg_ref, wu_ref, wd_ref, *rest, final_norm):
    if final_norm:
        fin_ref, o_ref, h_ref = rest
    else:
        o_ref, h_ref = rest
    j = pl.program_id(1)

    @pl.when(j == 0)
    def _():
        x = x_ref[...]
        h_ref[...] = (x * _rms_scale(x) * g_ref[...]).astype(BF16)
        o_ref[...] = jnp.zeros_like(o_ref)

    h = h_ref[...]
    gate = _dot(h, wg_ref[...])
    up = _dot(h, wu_ref[...])
    act = (gate * jax.nn.sigmoid(gate) * up).astype(BF16)
    o_ref[...] += _dot(act, wd_ref[...])

    @pl.when(j == pl.num_programs(1) - 1)
    def _():
        y = x_ref[...] + 0.5 * o_ref[...]
        if final_norm:
            y = y * _rms_scale(y) * fin_ref[...]
        o_ref[...] = y


def _ffn(x, g, w_up, w_down, fin=None, *, tm=512, tf=512):
    m, d = x.shape
    f = w_down.shape[0]
    tm = min(tm, m)
    nf = f // tf
    in_specs = [
        pl.BlockSpec((tm, d), lambda i, j: (i, 0)),
        pl.BlockSpec((1, d), lambda i, j: (0, 0)),
        pl.BlockSpec((d, tf), lambda i, j: (0, j)),
        pl.BlockSpec((d, tf), lambda i, j: (0, j + nf)),
        pl.BlockSpec((tf, d), lambda i, j: (j, 0)),
    ]
    args = [x, g.reshape(1, d), w_up, w_up, w_down]
    if fin is not None:
        in_specs.append(pl.BlockSpec((1, d), lambda i, j: (0, 0)))
        args.append(fin.reshape(1, d))
    return pl.pallas_call(
        functools.partial(_ffn_kernel, final_norm=fin is not None),
        grid=(m // tm, nf),
        in_specs=in_specs,
        out_specs=pl.BlockSpec((tm, d), lambda i, j: (i, 0)),
        out_shape=jax.ShapeDtypeStruct((m, d), F32),
        scratch_shapes=[pltpu.VMEM((tm, d), BF16)],
        compiler_params=_cparams("parallel", "arbitrary"),
        name="ffn",
    )(*args)


def _mm_kernel(*refs, n_b, n_row, n_tile, n_out, rmsnorm, epilogue):
    pos = 0
    a_ref = refs[pos]; pos += 1
    if rmsnorm:
        g_ref = refs[pos]; pos += 1
    b_refs = refs[pos:pos + n_b]; pos += n_b
    row_refs = refs[pos:pos + n_row]; pos += n_row
    tile_refs = refs[pos:pos + n_tile]; pos += n_tile
    out_refs = refs[pos:pos + n_out]; pos += n_out
    if rmsnorm:
        h_ref = refs[pos]

        @pl.when(pl.program_id(1) == 0)
        def _():
            x = a_ref[...]
            h_ref[...] = (x * _rms_scale(x) * g_ref[...]).astype(BF16)

        a = h_ref[...]
    else:
        a = a_ref[...]
    accs = [_dot(a, b_ref[...]) for b_ref in b_refs]
    outs = epilogue(accs, [r[...] for r in row_refs], [t[...] for t in tile_refs])
    for o_ref, o in zip(out_refs, outs):
        o_ref[...] = o.astype(o_ref.dtype)


def _mm(a, bs, epilogue, out_dtypes, *, n, rows=(), tiles=(), gain=None, tm=512, tn=None):
    m, k = a.shape
    tm = min(tm, m)
    if tn is None:
        tn = max(n // len(bs), 4 * LANES)
    tn = min(tn, n)
    rmsnorm = gain is not None
    in_specs = [pl.BlockSpec((tm, k), lambda i, j: (i, 0))]
    args = [a]
    if rmsnorm:
        in_specs.append(pl.BlockSpec((1, k), lambda i, j: (0, 0)))
        args.append(gain.reshape(1, k))
    for b, off in bs:
        ob = off // tn
        in_specs.append(pl.BlockSpec((k, tn), lambda i, j, ob=ob: (0, j + ob)))
        args.append(b)
    for r in rows:
        if r.shape[1] == tn and n != tn:
            in_specs.append(pl.BlockSpec((1, tn), lambda i, j: (0, 0)))
        else:
            in_specs.append(pl.BlockSpec((1, tn), lambda i, j: (0, j)))
        args.append(r)
    for t in tiles:
        in_specs.append(pl.BlockSpec((tm, tn), lambda i, j: (i, j)))
        args.append(t)
    out_shape = [jax.ShapeDtypeStruct((m, n), dt) for dt in out_dtypes]
    out_specs = [pl.BlockSpec((tm, tn), lambda i, j: (i, j)) for _ in out_dtypes]
    kern = functools.partial(
        _mm_kernel, n_b=len(bs), n_row=len(rows), n_tile=len(tiles), n_out=len(out_dtypes),
        rmsnorm=rmsnorm, epilogue=epilogue)
    return pl.pallas_call(
        kern,
        grid=(m // tm, n // tn),
        in_specs=in_specs,
        out_specs=out_specs,
        out_shape=out_shape,
        scratch_shapes=[pltpu.VMEM((tm, k), BF16)] if rmsnorm else [],
        compiler_params=_cparams("parallel", "arbitrary"),
        name="proj",
    )(*args)


def _fox_qkvg_epilogue(accs, rows, tiles):
    gq, gk = rows
    q, k, v, g = accs
    tn = q.shape[1]

    def head_norm(t, gain):
        parts = []
        for hh in range(tn // FOX_HEAD_DIM):
            sl = t[:, hh * FOX_HEAD_DIM:(hh + 1) * FOX_HEAD_DIM]
            parts.append(sl * _rms_scale(sl))
        return jnp.concatenate(parts, axis=1) * gain

    return head_norm(q, gq), head_norm(k, gk), v, jax.nn.sigmoid(g)


def _fox_forget_kernel(x_ref, g_ref, wf_ref, bf_ref, tri_ref, c_ref, carry_ref):
    i = pl.program_id(0)

    @pl.when(i == 0)
    def _():
        carry_ref[...] = jnp.zeros_like(carry_ref)

    x = x_ref[...]
    h = (x * _rms_scale(x) * g_ref[...]).astype(BF16)
    f_logit = _dot(wf_ref[...], h, NT_DIMS) + bf_ref[...]
    log_f = -_softplus(-f_logit) * LOG2_E
    tri = tri_ref[...]
    local = None
    for part in _split_bf16(log_f, 3):
        t = _dot(part, tri)
        local = t if local is None else local + t
    c = local + carry_ref[...]
    c_ref[...] = c
    carry_ref[...] = c[:, -1:]


def _fox_forget(x, g, wf_t, b_f, *, tm=512):
    m, d = x.shape
    nh = wf_t.shape[0]
    tm = min(tm, m)
    tri = jnp.triu(jnp.ones((tm, tm), F32)).astype(BF16)
    return pl.pallas_call(
        _fox_forget_kernel,
        grid=(m // tm,),
        in_specs=[
            pl.BlockSpec((tm, d), lambda i: (i, 0)),
            pl.BlockSpec((1, d), lambda i: (0, 0)),
            pl.BlockSpec((nh, d), lambda i: (0, 0)),
            pl.BlockSpec((nh, 1), lambda i: (0, 0)),
            pl.BlockSpec((tm, tm), lambda i: (0, 0)),
        ],
        out_specs=pl.BlockSpec((nh, tm), lambda i: (0, i)),
        out_shape=jax.ShapeDtypeStruct((nh, m), F32),
        scratch_shapes=[pltpu.VMEM((nh, 1), F32)],
        compiler_params=_cparams("arbitrary"),
        name="fox_forget",
    )(x, g.reshape(1, d), wf_t, b_f.reshape(nh, 1), tri)


def _fox_attn_kernel(q_ref, k_ref, v_ref, cq_ref, ck_ref, g_ref, o_ref, m_ref, acc_ref,
                     *, tq, tk, rb):
    i = pl.program_id(1)
    n_rb = tq // rb
    kpq = tq // tk
    m_ref[...] = jnp.full_like(m_ref, NEG_BIG)
    acc_ref[...] = jnp.zeros_like(acc_ref)
    c_first = cq_ref[0, 0][:, 0:1]
    dh = q_ref.shape[1]

    def step(j, row_blocks):
        ks = pl.multiple_of(j * tk, tk)
        k = k_ref[pl.ds(ks, tk), :]
        v = v_ref[pl.ds(ks, tk), :]
        v_aug = jnp.concatenate([v, jnp.ones_like(v)], axis=1)
        bias = c_first - ck_ref[0, j]
        for r, mask in row_blocks:
            rows = pl.ds(r * rb, rb)
            s = _dot(q_ref[rows, :], k, NT_DIMS) + bias
            if mask is not None:
                s = jnp.where(mask, s, NEG_BIG)
            m_prev = m_ref[rows, :]
            m_new = jnp.maximum(m_prev, jnp.max(s, axis=-1, keepdims=True))
            alpha = jnp.exp2(m_prev - m_new)
            pmat = jnp.exp2(s - jnp.concatenate([m_new] * (tk // dh), axis=1))
            acc_ref[rows, :] = (jnp.concatenate([alpha, alpha], axis=1) * acc_ref[rows, :]
                                + _dot(pmat.astype(BF16), v_aug))
            m_ref[rows, :] = m_new

    full = [(r, None) for r in range(n_rb)]

    def make_body(n_blocks, first):
        def body(jo, carry):
            for jj in range(n_blocks):
                step(first + jo * n_blocks + jj, full)
            return carry
        return body

    n_full = i * kpq
    n_big = n_full // FOX_BLOCKS_PER_TRIP
    lax.fori_loop(0, n_big, make_body(FOX_BLOCKS_PER_TRIP, 0), 0)
    n_done = n_big * FOX_BLOCKS_PER_TRIP
    lax.fori_loop(0, (n_full - n_done) // kpq, make_body(kpq, n_done), 0)
    for jj in range(kpq):
        blocks = []
        for r in range(n_rb):
            r0, r1 = r * rb, (r + 1) * rb - 1
            c0, c1 = jj * tk, (jj + 1) * tk - 1
            if r1 < c0:
                continue
            if r0 >= c1:
                blocks.append((r, None))
            else:
                row = r0 + lax.broadcasted_iota(jnp.int32, (rb, tk), 0)
                col = c0 + lax.broadcasted_iota(jnp.int32, (rb, tk), 1)
                blocks.append((r, col <= row))
        step(i * kpq + jj, blocks)
    o = acc_ref[:, :dh] / acc_ref[:, dh:]
    o_ref[...] = (o * g_ref[...].astype(F32)).astype(o_ref.dtype)


def _fox_attention(q, k, v, c, gate, *, tq=2048, tk=512, rb=256):
    m, d = q.shape
    nh = d // FOX_HEAD_DIM
    tq = min(tq, m)
    tk = min(tk, tq)
    rb = min(rb, tq)
    dh = FOX_HEAD_DIM
    cq = c.reshape(nh, m // tq, 1, tq)
    ck = c.reshape(nh, m // tk, 1, tk)
    return pl.pallas_call(
        functools.partial(_fox_attn_kernel, tq=tq, tk=tk, rb=rb),
        grid=(nh, m // tq),
        in_specs=[
            pl.BlockSpec((tq, dh), lambda h, i: (i, h)),
            pl.BlockSpec((m, dh), lambda h, i: (0, h)),
            pl.BlockSpec((m, dh), lambda h, i: (0, h)),
            pl.BlockSpec((1, 1, 1, tq), lambda h, i: (h, i, 0, 0)),
            pl.BlockSpec((1, m // tk, 1, tk), lambda h, i: (h, 0, 0, 0)),
            pl.BlockSpec((tq, dh), lambda h, i: (i, h)),
        ],
        out_specs=pl.BlockSpec((tq, dh), lambda h, i: (i, h)),
        out_shape=jax.ShapeDtypeStruct((m, d), BF16),
        scratch_shapes=[
            pltpu.VMEM((tq, dh), F32),
            pltpu.VMEM((tq, 2 * dh), F32),
        ],
        compiler_params=_cparams("parallel", "arbitrary"),
        name="fox_attn",
    )(q, k, v, cq, ck, gate)


def _fox_layer(x, norm_g, w_in, b_f, qk_gain, w_out):
    m, d = x.shape
    nh = d // FOX_HEAD_DIM
    w_in_b = w_in.astype(BF16)
    tn = 512
    scale = FOX_HEAD_DIM ** -0.5 * LOG2_E
    gq = jnp.tile(qk_gain[0] * scale, tn // FOX_HEAD_DIM).reshape(1, tn)
    gk = jnp.tile(qk_gain[1], tn // FOX_HEAD_DIM).reshape(1, tn)
    q, k, v, gate = _mm(
        x, [(w_in_b, 0), (w_in_b, d), (w_in_b, 2 * d), (w_in_b, 3 * d)],
        _fox_qkvg_epilogue, [BF16, BF16, BF16, BF16], n=d, rows=[gq, gk], gain=norm_g, tn=tn)
    wf_t = w_in[:, 4 * d:].T.astype(BF16)
    c = _fox_forget(x, norm_g, wf_t, b_f)
    og = _fox_attention(q, k, v, c, gate)
    (x_new,) = _mm(og, [(w_out.astype(BF16), 0)],
                   lambda accs, rows, tiles: (tiles[0] + accs[0],), [F32], n=d, tiles=[x])
    return x_new


def _rwkv_prep_kernel(x_ref, xp_ref, g_ref, mu_ref, *rest):
    out_refs = rest[:6]
    hbuf = rest[6]
    i = pl.program_id(0)
    tm = x_ref.shape[0]
    x = x_ref[...]
    g = g_ref[...]
    h = x * _rms_scale(x) * g
    xp = xp_ref[...][7:8, :]
    hp = xp * _rms_scale(xp) * g
    hp = jnp.where(i == 0, jnp.zeros_like(hp), hp)
    hbuf[pl.ds(8, tm), :] = h
    hbuf[pl.ds(7, 1), :] = hp
    xx = hbuf[pl.ds(7, tm), :] - h
    mu = mu_ref[...]
    for n, o_ref in enumerate(out_refs):
        o_ref[...] = (h + xx * mu[n:n + 1, :]).astype(o_ref.dtype)


def _rwkv_prep(x, g, mu, *, tm=256):
    m, d = x.shape
    tm = min(tm, m)
    rb = tm // 8
    return pl.pallas_call(
        _rwkv_prep_kernel,
        grid=(m // tm,),
        in_specs=[
            pl.BlockSpec((tm, d), lambda i: (i, 0)),
            pl.BlockSpec((8, d), lambda i: (jnp.maximum(i * rb - 1, 0), 0)),
            pl.BlockSpec((1, d), lambda i: (0, 0)),
            pl.BlockSpec((8, d), lambda i: (0, 0)),
        ],
        out_specs=[pl.BlockSpec((tm, d), lambda i: (i, 0)) for _ in range(6)],
        out_shape=[jax.ShapeDtypeStruct((m, d), BF16) for _ in range(6)],
        scratch_shapes=[pltpu.VMEM((tm + 8, d), F32)],
        compiler_params=_cparams("parallel"),
        name="rwkv_prep",
    )(x, x, g.reshape(1, d), jnp.pad(mu, ((0, 2), (0, 0))))


def _lora_kernel(x_ref, w1_ref, w2_ref, b_ref, o_ref, *, mid_act, out_act):
    t = _dot(x_ref[...], w1_ref[...])
    t = mid_act(t).astype(BF16)
    y = _dot(t, w2_ref[...]) + b_ref[...]
    o_ref[...] = out_act(y).astype(o_ref.dtype)


def _lora(x, w1, w2, bias, mid_act, out_act, *, tm=512):
    m, d = x.shape
    r = w1.shape[1]
    n = w2.shape[1]
    tm = min(tm, m)
    return pl.pallas_call(
        functools.partial(_lora_kernel, mid_act=mid_act, out_act=out_act),
        grid=(m // tm,),
        in_specs=[
            pl.BlockSpec((tm, d), lambda i: (i, 0)),
            pl.BlockSpec((d, r), lambda i: (0, 0)),
            pl.BlockSpec((r, n), lambda i: (0, 0)),
            pl.BlockSpec((1, n), lambda i: (0, 0)),
        ],
        out_specs=pl.BlockSpec((tm, n), lambda i: (i, 0)),
        out_shape=jax.ShapeDtypeStruct((m, n), F32),
        compiler_params=_cparams("parallel"),
        name="rwkv_lora",
    )(x, w1.astype(BF16), w2.astype(BF16), bias.reshape(1, n))


def _bdot(a, b, spec, passes=3):
    ein = lambda x, y: jnp.einsum(spec, x, y, preferred_element_type=F32)
    if passes == 1:
        return ein(a.astype(BF16), b.astype(BF16))
    a_hi, a_lo = _split_bf16(a, 2)
    b_hi, b_lo = _split_bf16(b, 2)
    return ein(a_hi, b_hi) + ein(a_hi, b_lo) + ein(a_lo, b_hi)


def _rwkv_core_kernel(r_ref, k_ref, v_ref, ld_ref, a_ref, g_ref, kk_ref, ka_ref, rk_ref,
                      lnw_ref, lnb_ref, tri_ref, o_ref, st_ref, *, chunk):
    t_idx = pl.program_id(1)

    @pl.when(t_idx == 0)
    def _():
        st_ref[...] = jnp.zeros_like(st_ref)

    tt = r_ref.shape[0]
    c = chunk
    nc = tt // c
    n = RWKV_HEAD_DIM
    shape3 = (nc, c, LANES)
    row_l = lax.broadcasted_iota(jnp.int32, (LANES, LANES), 0)
    col_l = lax.broadcasted_iota(jnp.int32, (LANES, LANES), 1)
    same_head = (row_l < n) == (col_l < n)
    seg_ones = same_head.astype(BF16)
    bd_mask = same_head.astype(F32)
    eye_l = (row_l == col_l).astype(F32)

    def seg_sum(x):
        acc = None
        for part in _split_bf16(x.reshape(tt, LANES), 2):
            t = _dot(part, seg_ones)
            acc = t if acc is None else acc + t
        return acc.reshape(shape3)

    r = r_ref[...].reshape(shape3)
    k = k_ref[...].reshape(shape3)
    v = v_ref[...].reshape(shape3)
    ld = ld_ref[...]
    a = a_ref[...].reshape(shape3)

    lane = lax.broadcasted_iota(jnp.int32, (1, 1, LANES), 2)
    head0 = lane < n
    m0 = head0.astype(F32)
    m1 = 1.0 - m0

    kk = k * kk_ref[...]
    kk = kk / jnp.maximum(jnp.sqrt(seg_sum(kk * kk)), RWKV_NORM_EPS)
    k2 = k * (1.0 + (a - 1.0) * ka_ref[...])
    av = -kk
    bv = kk * a
    gam = None
    ld = ld.reshape(shape3)
    tri = jnp.broadcast_to(tri_ref[...][None], (nc, c, c))
    for part in _split_bf16(ld, 3):
        t = jnp.einsum('bts,bsl->btl', tri, part, preferred_element_type=F32)
        gam = t if gam is None else gam + t
    g_end = gam[:, c - 1:c, :]
    at = av * jnp.exp(gam - ld)
    rt = r * jnp.exp(gam)
    e_neg = jnp.exp(-gam)
    bt = bv * e_neg
    kt = k2 * e_neg
    e_end = jnp.exp(g_end - gam)
    b_end = bv * e_end
    k_end = k2 * e_end

    both = lambda x: jnp.concatenate([x, x], axis=0)
    own = lambda x: jnp.where(head0, x[:nc], x[nc:])
    lhs = both(jnp.concatenate([at, rt], axis=1))
    gb = _bdot(lhs, jnp.concatenate([bt * m0, bt * m1], axis=0), 'bqd,bkd->bqk')
    gk = _bdot(lhs, jnp.concatenate([kt * m0, kt * m1], axis=0), 'bqd,bkd->bqk')
    t_row = lax.broadcasted_iota(jnp.int32, (1, c, c), 1)
    s_col = lax.broadcasted_iota(jnp.int32, (1, c, c), 2)
    strict = s_col < t_row
    incl = s_col <= t_row
    g_ab = jnp.where(strict, gb[:, :c], 0.0)
    g_rb = jnp.where(incl, gb[:, c:], 0.0)
    g_ak = jnp.where(strict, gk[:, :c], 0.0)
    g_rk = jnp.where(incl, gk[:, c:], 0.0)
    tinv = (s_col == t_row).astype(F32) + g_ab
    npow = g_ab
    for _ in range(int(math.log2(c)) - 1):
        npow = _bdot(npow, npow, 'bij,bjk->bik', passes=1)
        tinv = tinv + _bdot(tinv, npow, 'bij,bjk->bik', passes=1)
    apply = lambda w, x: _bdot(w, x, 'bts,bsl->btl', passes=1)
    outer = lambda x, y: _bdot(x, y, 'bti,btj->bij', passes=1)
    v2 = both(v)
    av_v = own(apply(g_ak, v2))
    pmat = own(apply(tinv, both(at)))
    qmat = own(apply(tinv, both(av_v)))
    m_mat = outer(b_end, pmat) * bd_mask + eye_l * jnp.exp(g_end)
    n_mat = (outer(b_end, qmat) + outer(k_end, v)) * bd_mask
    o1 = rt + own(apply(g_rb, both(pmat)))
    o2 = own(apply(g_rb, both(qmat)) + apply(g_rk, v2))

    npair = nc // 2
    pairs = lambda x: x.reshape((npair, 2) + x.shape[1:])
    m_ab, n_ab = pairs(m_mat), pairs(n_mat)
    m_a, m_b, n_a, n_b = m_ab[:, 0], m_ab[:, 1], n_ab[:, 0], n_ab[:, 1]
    m_pair = apply(m_b, m_a)
    n_pair = apply(m_b, n_a) + n_b
    st = st_ref[...]
    st_a = []
    for p in range(npair):
        st_a.append(st)
        st = _dot_f32(m_pair[p], st, passes=1) + n_pair[p]
    st_ref[...] = st
    st_a = jnp.stack(st_a, axis=0)
    st_b = apply(m_a, st_a) + n_a
    st_all = jnp.stack([st_a, st_b], axis=1).reshape(nc, LANES, LANES)
    o = apply(o1, st_all) + o2

    inv_n = 1.0 / n
    mean = seg_sum(o) * inv_n
    cen = o - mean
    var = seg_sum(cen * cen) * inv_n
    y = cen * lax.rsqrt(var + RWKV_LN_EPS) * lnw_ref[...] + lnb_ref[...]
    bonus = seg_sum(r * k2 * rk_ref[...]) * v
    out = (y + bonus) * g_ref[...].reshape(shape3)
    o_ref[...] = out.reshape(tt, LANES).astype(o_ref.dtype)


def _rwkv_core(r, k, v, ld, a, g, k_k, k_a, r_k, ln_w, ln_b, *, tt=1024):
    m, d = r.shape
    tt = min(tt, m)
    chunk = min(RWKV_CHUNK, tt)
    tri = jnp.tril(jnp.ones((chunk, chunk), F32)).astype(BF16)
    seq = pl.BlockSpec((tt, LANES), lambda hp, t: (t, hp))
    par = pl.BlockSpec((1, LANES), lambda hp, t: (0, hp))
    return pl.pallas_call(
        functools.partial(_rwkv_core_kernel, chunk=chunk),
        grid=(d // LANES, m // tt),
        in_specs=[seq] * 6 + [par] * 5 + [pl.BlockSpec((chunk, chunk), lambda hp, t: (0, 0))],
        out_specs=seq,
        out_shape=jax.ShapeDtypeStruct((m, d), BF16),
        scratch_shapes=[pltpu.VMEM((LANES, LANES), F32)],
        compiler_params=_cparams("parallel", "arbitrary"),
        name="rwkv_core",
    )(r, k, v, ld, a, g, k_k.reshape(1, d), k_a.reshape(1, d), r_k.reshape(1, d),
      ln_w.reshape(1, d), ln_b.reshape(1, d), tri)


def _rwkv_layer(x, norm_g, mu, w_rkv, w0, w1, w2, a0, a1, a2, g1, g2, k_k, k_a, r_k, ln_w, ln_b, w_out):
    m, d = x.shape
    xr, xw, xk, xv, xa, xg = _rwkv_prep(x, norm_g, mu)
    w_rkv_b = w_rkv.astype(BF16)
    plain = lambda accs, rows, tiles: (accs[0],)
    (r,) = _mm(xr, [(w_rkv_b[0], 0)], plain, [F32], n=d)
    (k,) = _mm(xk, [(w_rkv_b[1], 0)], plain, [F32], n=d)
    (v,) = _mm(xv, [(w_rkv_b[2], 0)], plain, [F32], n=d)
    ident = lambda t: t
    ld = _lora(xw, w1, w2, w0, jnp.tanh, lambda y: -jnp.exp(-_softplus(-y) - 0.5))
    a = _lora(xa, a1, a2, a0, ident, jax.nn.sigmoid)
    g = _lora(xg, g1, g2, jnp.zeros((d,), F32), jax.nn.sigmoid, ident)
    y = _rwkv_core(r, k, v, ld, a, g, k_k, k_a, r_k, ln_w, ln_b)
    (x_new,) = _mm(y, [(w_out.astype(BF16), 0)],
                   lambda accs, rows, tiles: (tiles[0] + accs[0],), [F32], n=d, tiles=[x])
    return x_new


def _s5_core_kernel(u_ref, b_ref, tab_ref, c_ref, d_ref, o_ref, hbuf, cbuf, hb16, carry_ref):
    t_idx = pl.program_id(1)
    tl = u_ref.shape[0]
    width = hbuf.shape[1]
    half = width // 2
    rc = tab_ref.shape[2]
    n_lvl = len(S5_LOCAL_SHIFTS)
    re = pl.ds(0, half)
    im = pl.ds(half, half)

    @pl.when(t_idx == 0)
    def _():
        carry_ref[...] = jnp.zeros_like(carry_ref)

    u = u_ref[...]
    hbuf[...] = _dot(u.astype(BF16), b_ref[0])

    def local(n, carry):
        rows = pl.ds(pl.multiple_of(n * rc, rc), rc)
        xr = hbuf[rows, re]
        xi = hbuf[rows, im]
        for lvl, s in enumerate(S5_LOCAL_SHIFTS):
            tr = tab_ref[0, lvl, :, re]
            ti = tab_ref[0, lvl, :, im]
            sr = pltpu.roll(xr, s, axis=0)
            si = pltpu.roll(xi, s, axis=0)
            xr, xi = xr + tr * sr - ti * si, xi + tr * si + ti * sr
        hbuf[rows, re] = xr
        hbuf[rows, im] = xi
        return carry

    lax.fori_loop(0, tl // rc, local, 0)

    a8r = tab_ref[0, n_lvl, pl.ds(7, 1), re]
    a8i = tab_ref[0, n_lvl, pl.ds(7, 1), im]
    cr = carry_ref[0:1, re]
    ci = carry_ref[0:1, im]
    for g in range(tl // 8):
        cbuf[pl.ds(8 * g, 8), re] = jnp.broadcast_to(cr, (8, half))
        cbuf[pl.ds(8 * g, 8), im] = jnp.broadcast_to(ci, (8, half))
        lr = hbuf[pl.ds(8 * g + 7, 1), re]
        li = hbuf[pl.ds(8 * g + 7, 1), im]
        cr, ci = lr + a8r * cr - a8i * ci, li + a8r * ci + a8i * cr
    carry_ref[0:1, re] = cr
    carry_ref[0:1, im] = ci

    def apply(n, carry):
        rows = pl.ds(pl.multiple_of(n * rc, rc), rc)
        pr = tab_ref[0, n_lvl, :, re]
        pi = tab_ref[0, n_lvl, :, im]
        br = cbuf[rows, re]
        bi = cbuf[rows, im]
        hb16[rows, re] = (hbuf[rows, re] + pr * br - pi * bi).astype(BF16)
        hb16[rows, im] = (hbuf[rows, im] + pr * bi + pi * br).astype(BF16)
        return carry

    lax.fori_loop(0, tl // rc, apply, 0)
    y = _dot(hb16[...], c_ref[0]) + d_ref[...] * u
    o_ref[...] = jax.nn.gelu(y).astype(o_ref.dtype)


def _s5_core(u, b_blk, tab, c_blk, d_skip, *, tl=512):
    m, d = u.shape
    tl = min(tl, m)
    nb = d // LANES
    width = b_blk.shape[2]
    n_tab, rc = tab.shape[1], tab.shape[2]
    return pl.pallas_call(
        _s5_core_kernel,
        grid=(nb, m // tl),
        in_specs=[
            pl.BlockSpec((tl, LANES), lambda gb, t: (t, gb)),
            pl.BlockSpec((1, LANES, width), lambda gb, t: (gb, 0, 0)),
            pl.BlockSpec((1, n_tab, rc, width), lambda gb, t: (gb, 0, 0, 0)),
            pl.BlockSpec((1, width, LANES), lambda gb, t: (gb, 0, 0)),
            pl.BlockSpec((1, LANES), lambda gb, t: (0, gb)),
        ],
        out_specs=pl.BlockSpec((tl, LANES), lambda gb, t: (t, gb)),
        out_shape=jax.ShapeDtypeStruct((m, d), BF16),
        scratch_shapes=[pltpu.VMEM((tl, width), F32), pltpu.VMEM((tl, width), F32),
                        pltpu.VMEM((tl, width), BF16), pltpu.VMEM((8, width), F32)],
        compiler_params=_cparams("parallel", "arbitrary"),
        name="s5_core",
    )(u, b_blk, tab, c_blk, d_skip.reshape(1, d))


def _s5_tables(lam_re, lam_im, log_step, b_re, b_im, c_re, c_im):
    g, p = lam_re.shape
    q = b_re.shape[2]
    gpb = S5_GROUPS_PER_BLOCK
    nb = g // gpb
    lr = jnp.minimum(lam_re.astype(F32), S5_MAX_RE)
    li = lam_im.astype(F32)
    dt = jnp.exp(log_step.astype(F32))[:, None]
    mag = jnp.exp(lr * dt)
    abar_re, abar_im = mag * jnp.cos(li * dt), mag * jnp.sin(li * dt)
    den = lr * lr + li * li
    nr, ni = abar_re - 1.0, abar_im
    q_re, q_im = (nr * lr + ni * li) / den, (ni * lr - nr * li) / den
    br, bi = b_re.astype(F32), b_im.astype(F32)
    bbar_re = q_re[..., None] * br - q_im[..., None] * bi
    bbar_im = q_re[..., None] * bi + q_im[..., None] * br
    eye = jnp.eye(gpb, dtype=F32)

    def blockdiag_in(bb):
        t = bb.reshape(nb, gpb, p, q).transpose(0, 1, 3, 2)
        return jnp.einsum('ngqp,gh->ngqhp', t, eye).reshape(nb, gpb * q, gpb * p)

    def blockdiag_out(cc):
        t = cc.reshape(nb, gpb, q, p).transpose(0, 1, 3, 2)
        return jnp.einsum('ngpq,gh->ngphq', t, eye).reshape(nb, gpb * p, gpb * q)

    b_blk = jnp.concatenate([blockdiag_in(bbar_re), blockdiag_in(bbar_im)], axis=2).astype(BF16)
    c_blk = jnp.concatenate([blockdiag_out(c_re.astype(F32)), -blockdiag_out(c_im.astype(F32))],
                            axis=1).astype(BF16)
    ar = abar_re.reshape(nb, gpb * p)
    ai = abar_im.reshape(nb, gpb * p)
    pows = [(ar, ai)]
    for _ in range(7):
        pr, pi = pows[-1]
        pows.append((pr * ar - pi * ai, pr * ai + pi * ar))
    row = jnp.arange(S5_SCAN_ROWS) % 8

    def table(vals):
        zero = jnp.zeros_like(ar)
        re = jnp.stack([vals[r][0] if vals[r] is not None else zero for r in range(8)], axis=1)
        im = jnp.stack([vals[r][1] if vals[r] is not None else zero for r in range(8)], axis=1)
        return jnp.concatenate([re[:, row], im[:, row]], axis=2)

    tabs = [table([pows[s - 1] if r >= s else None for r in range(8)]) for s in S5_LOCAL_SHIFTS]
    tabs.append(table([pows[r] for r in range(8)]))
    return b_blk, jnp.stack(tabs, axis=1), c_blk


def _s5_layer(x, norm_g, w_in, lam_re, lam_im, log_step, b_re, b_im, c_re, c_im, d_skip, w_out):
    m, d = x.shape
    (u,) = _mm(x, [(w_in.astype(BF16), 0)], lambda accs, rows, tiles: (accs[0],), [F32],
               n=d, gain=norm_g)
    b_blk, a_blk, c_blk = _s5_tables(lam_re, lam_im, log_step, b_re, b_im, c_re, c_im)
    y = _s5_core(u, b_blk, a_blk, c_blk, d_skip)
    w_out_b = w_out.astype(BF16)
    (x_new,) = _mm(y, [(w_out_b, 0), (w_out_b, d)],
                   lambda accs, rows, tiles: (tiles[0] + accs[0] * jax.nn.sigmoid(accs[1]),),
                   [F32], n=d, tiles=[x])
    return x_new


def kernel(x, norm_w, ffn_w_up, ffn_w_down, fox_w_in, fox_b_f, fox_qk_gain, fox_w_out, rwkv_mu, rwkv_w_rkv, rwkv_w0, rwkv_w1, rwkv_w2, rwkv_a0, rwkv_a1, rwkv_a2, rwkv_g1, rwkv_g2, rwkv_k_k, rwkv_k_a, rwkv_r_k, rwkv_ln_w, rwkv_ln_b, rwkv_w_out, s5_w_in, s5_lam_re, s5_lam_im, s5_log_step, s5_b_re, s5_b_im, s5_c_re, s5_c_im, s5_d, s5_w_out, final_norm):
    bsz, s, d = x.shape
    depth = norm_w.shape[0]
    outs = []
    for b in range(bsz):
        xb = x[b]
        ia = ib = ic = 0
        for i in range(depth):
            xb = _ffn(xb, norm_w[i, 0], ffn_w_up[i, 0].astype(BF16), ffn_w_down[i, 0].astype(BF16))
            mixer = i % 3
            if mixer == 0:
                xb = _fox_layer(xb, norm_w[i, 1], fox_w_in[ia], fox_b_f[ia], fox_qk_gain[ia], fox_w_out[ia])
                ia += 1
            elif mixer == 1:
                xb = _rwkv_layer(xb, norm_w[i, 1], rwkv_mu[ib], rwkv_w_rkv[ib], rwkv_w0[ib], rwkv_w1[ib],
                                 rwkv_w2[ib], rwkv_a0[ib], rwkv_a1[ib], rwkv_a2[ib], rwkv_g1[ib], rwkv_g2[ib],
                                 rwkv_k_k[ib], rwkv_k_a[ib], rwkv_r_k[ib].reshape(-1), rwkv_ln_w[ib],
                                 rwkv_ln_b[ib], rwkv_w_out[ib])
                ib += 1
            else:
                xb = _s5_layer(xb, norm_w[i, 1], s5_w_in[ic], s5_lam_re[ic], s5_lam_im[ic], s5_log_step[ic],
                               s5_b_re[ic], s5_b_im[ic], s5_c_re[ic], s5_c_im[ic], s5_d[ic], s5_w_out[ic])
                ic += 1
            fin = final_norm if i == depth - 1 else None
            xb = _ffn(xb, norm_w[i, 2], ffn_w_up[i, 1].astype(BF16), ffn_w_down[i, 1].astype(BF16), fin)
        outs.append(xb)
    return jnp.stack(outs, axis=0)
```

```python
import functools
import math

import jax
import jax.numpy as jnp
from jax import lax
from jax.experimental import pallas as pl
from jax.experimental.pallas import tpu as pltpu

F32 = jnp.float32
BF16 = jnp.bfloat16

V7X_VMEM_BYTES = 64 * 1024 * 1024
VMEM_LIMIT_BYTES = V7X_VMEM_BYTES - 8 * 1024 * 1024
LANES = 128

RMS_EPS = 1e-6
FOX_HEAD_DIM = 128
FOX_BLOCKS_PER_TRIP = 4
RWKV_HEAD_DIM = 64
RWKV_LN_EPS = 64e-5
RWKV_NORM_EPS = 1e-12
RWKV_CHUNK = 64
RWKV_SWEEP_LEVELS = 2
S5_GROUP = 16
S5_STATE = 64
S5_MAX_RE = -1e-4
S5_GROUPS_PER_BLOCK = LANES // S5_GROUP
NEG_BIG = -1e30
LOG2_E = math.log2(math.e)
S5_SCAN_ROWS = 32
S5_LOCAL_SHIFTS = (1, 2, 4)

NT_DIMS = (((1,), (1,)), ((), ()))
TN_DIMS = (((0,), (0,)), ((), ()))
NN_DIMS = (((1,), (0,)), ((), ()))


def _cparams(*sem, vmem_limit_bytes=VMEM_LIMIT_BYTES):
    return pltpu.CompilerParams(dimension_semantics=sem, vmem_limit_bytes=vmem_limit_bytes)


def _rms_scale(x):
    return lax.rsqrt(jnp.mean(x * x, axis=-1, keepdims=True) + RMS_EPS)


def _softplus(z):
    return jnp.maximum(z, 0.0) + jnp.log1p(jnp.exp(-jnp.abs(z)))


def _dot(a, b, dims=NN_DIMS):
    return lax.dot_general(a, b, dims, preferred_element_type=F32)


def _split_bf16(x, parts):
    out = []
    rem = x
    for _ in range(parts):
        p = rem.astype(BF16)
        out.append(p)
        rem = rem - p.astype(F32)
    return out


def _dot_f32(a, b, dims=NN_DIMS, passes=3):
    if passes == 1:
        return _dot(a.astype(BF16), b.astype(BF16), dims)
    a_p = _split_bf16(a, 2 if passes == 3 else 3)
    b_p = _split_bf16(b, 2 if passes == 3 else 3)
    acc = None
    for ia, ap in enumerate(a_p):
        for ib, bp in enumerate(b_p):
            if ia + ib >= len(a_p):
                continue
            t = _dot(ap, bp, dims)
            acc = t if acc is None else acc + t
    return acc


def _ffn_kernel(x_ref, g_ref, wg_ref, wu_ref, wd_ref, *rest, final_norm):
    if final_norm:
        fin_ref, o_ref, h_ref = rest
    else:
        o_ref, h_ref = rest
    j = pl.program_id(1)

    @pl.when(j == 0)
    def _():
        x = x_ref[...]
        h_ref[...] = (x * _rms_scale(x) * g_ref[...]).astype(BF16)
        o_ref[...] = jnp.zeros_like(o_ref)

    h = h_ref[...]
    gate = _dot(h, wg_ref[0])
    up = _dot(h, wu_ref[0])
    act = (gate * jax.nn.sigmoid(gate) * up).astype(BF16)
    o_ref[...] += _dot(act, wd_ref[...])

    @pl.when(j == pl.num_programs(1) - 1)
    def _():
        y = x_ref[...] + 0.5 * o_ref[...]
        if final_norm:
            y = y * _rms_scale(y) * fin_ref[...]
        o_ref[...] = y


def _ffn(x, g, w_up, w_down, fin=None, *, tm=512, tf=512):
    m, d = x.shape
    f = w_down.shape[0]
    tm = min(tm, m)
    nf = f // tf
    w_up_blocks = w_up.astype(BF16).reshape(d, 2 * nf, tf).transpose(1, 0, 2)
    in_specs = [
        pl.BlockSpec((tm, d), lambda i, j: (i, 0)),
        pl.BlockSpec((1, d), lambda i, j: (0, 0)),
        pl.BlockSpec((1, d, tf), lambda i, j: (j, 0, 0)),
        pl.BlockSpec((1, d, tf), lambda i, j: (j + nf, 0, 0)),
        pl.BlockSpec((tf, d), lambda i, j: (j, 0)),
    ]
    args = [x, g.reshape(1, d), w_up_blocks, w_up_blocks, w_down.astype(BF16)]
    if fin is not None:
        in_specs.append(pl.BlockSpec((1, d), lambda i, j: (0, 0)))
        args.append(fin.reshape(1, d))
    return pl.pallas_call(
        functools.partial(_ffn_kernel, final_norm=fin is not None),
        grid=(m // tm, nf),
        in_specs=in_specs,
        out_specs=pl.BlockSpec((tm, d), lambda i, j: (i, 0)),
        out_shape=jax.ShapeDtypeStruct((m, d), F32),
        scratch_shapes=[pltpu.VMEM((tm, d), BF16)],
        compiler_params=_cparams("parallel", "arbitrary"),
        name="ffn",
    )(*args)


def _mm_kernel(*refs, n_b, n_row, n_tile, n_out, rmsnorm, epilogue):
    pos = 0
    a_ref = refs[pos]; pos += 1
    if rmsnorm:
        g_ref = refs[pos]; pos += 1
    b_refs = refs[pos:pos + n_b]; pos += n_b
    row_refs = refs[pos:pos + n_row]; pos += n_row
    tile_refs = refs[pos:pos + n_tile]; pos += n_tile
    out_refs = refs[pos:pos + n_out]; pos += n_out
    if rmsnorm:
        h_ref = refs[pos]

        @pl.when(pl.program_id(1) == 0)
        def _():
            x = a_ref[...]
            h_ref[...] = (x * _rms_scale(x) * g_ref[...]).astype(BF16)

        a = h_ref[...]
    else:
        a = a_ref[...]
    accs = [_dot(a, b_ref[...]) for b_ref in b_refs]
    outs = epilogue(accs, [r[...] for r in row_refs], [t[...] for t in tile_refs])
    for o_ref, o in zip(out_refs, outs):
        o_ref[...] = o.astype(o_ref.dtype)


def _mm(a, bs, epilogue, out_dtypes, *, n, rows=(), tiles=(), gain=None, tm=512, tn=None):
    m, k = a.shape
    tm = min(tm, m)
    if tn is None:
        tn = max(n // len(bs), 4 * LANES)
    tn = min(tn, n)
    rmsnorm = gain is not None
    in_specs = [pl.BlockSpec((tm, k), lambda i, j: (i, 0))]
    args = [a]
    if rmsnorm:
        in_specs.append(pl.BlockSpec((1, k), lambda i, j: (0, 0)))
        args.append(gain.reshape(1, k))
    for b, off in bs:
        ob = off // tn
        in_specs.append(pl.BlockSpec((k, tn), lambda i, j, ob=ob: (0, j + ob)))
        args.append(b)
    for r in rows:
        if r.shape[1] == tn and n != tn:
            in_specs.append(pl.BlockSpec((1, tn), lambda i, j: (0, 0)))
        else:
            in_specs.append(pl.BlockSpec((1, tn), lambda i, j: (0, j)))
        args.append(r)
    for t in tiles:
        in_specs.append(pl.BlockSpec((tm, tn), lambda i, j: (i, j)))
        args.append(t)
    out_shape = [jax.ShapeDtypeStruct((m, n), dt) for dt in out_dtypes]
    out_specs = [pl.BlockSpec((tm, tn), lambda i, j: (i, j)) for _ in out_dtypes]
    kern = functools.partial(
        _mm_kernel, n_b=len(bs), n_row=len(rows), n_tile=len(tiles), n_out=len(out_dtypes),
        rmsnorm=rmsnorm, epilogue=epilogue)
    return pl.pallas_call(
        kern,
        grid=(m // tm, n // tn),
        in_specs=in_specs,
        out_specs=out_specs,
        out_shape=out_shape,
        scratch_shapes=[pltpu.VMEM((tm, k), BF16)] if rmsnorm else [],
        compiler_params=_cparams("parallel", "arbitrary"),
        name="proj",
    )(*args)


def _fox_qkvg_epilogue(accs, rows, tiles):
    gq, gk = rows
    q, k, v, g = accs
    tn = q.shape[1]

    def head_norm(t, gain):
        parts = []
        for hh in range(tn // FOX_HEAD_DIM):
            sl = t[:, hh * FOX_HEAD_DIM:(hh + 1) * FOX_HEAD_DIM]
            parts.append(sl * _rms_scale(sl))
        return jnp.concatenate(parts, axis=1) * gain

    return head_norm(q, gq), head_norm(k, gk), v, jax.nn.sigmoid(g)


def _fox_forget_kernel(x_ref, g_ref, wf_ref, bf_ref, tri_ref, c_ref, carry_ref):
    i = pl.program_id(0)

    @pl.when(i == 0)
    def _():
        carry_ref[...] = jnp.zeros_like(carry_ref)

    x = x_ref[...]
    h = (x * _rms_scale(x) * g_ref[...]).astype(BF16)
    f_logit = _dot(wf_ref[...], h, NT_DIMS) + bf_ref[...]
    log_f = -_softplus(-f_logit) * LOG2_E
    tri = tri_ref[...]
    local = None
    for part in _split_bf16(log_f, 3):
        t = _dot(part, tri)
        local = t if local is None else local + t
    c = local + carry_ref[...]
    c_ref[...] = c
    carry_ref[...] = c[:, -1:]


def _fox_forget(x, g, wf_t, b_f, *, tm=512):
    m, d = x.shape
    nh = wf_t.shape[0]
    tm = min(tm, m)
    tri = jnp.triu(jnp.ones((tm, tm), F32)).astype(BF16)
    return pl.pallas_call(
        _fox_forget_kernel,
        grid=(m // tm,),
        in_specs=[
            pl.BlockSpec((tm, d), lambda i: (i, 0)),
            pl.BlockSpec((1, d), lambda i: (0, 0)),
            pl.BlockSpec((nh, d), lambda i: (0, 0)),
            pl.BlockSpec((nh, 1), lambda i: (0, 0)),
            pl.BlockSpec((tm, tm), lambda i: (0, 0)),
        ],
        out_specs=pl.BlockSpec((nh, tm), lambda i: (0, i)),
        out_shape=jax.ShapeDtypeStruct((nh, m), F32),
        scratch_shapes=[pltpu.VMEM((nh, 1), F32)],
        compiler_params=_cparams("arbitrary"),
        name="fox_forget",
    )(x, g.reshape(1, d), wf_t, b_f.reshape(nh, 1), tri)


def _fox_attn_kernel(q_ref, k_ref, v_ref, cq_ref, ck_ref, g_ref, o_ref, m_ref, acc_ref,
                     *, tq, tk, rb):
    i = pl.program_id(1)
    n_rb = tq // rb
    kpq = tq // tk
    m_ref[...] = jnp.full_like(m_ref, NEG_BIG)
    acc_ref[...] = jnp.zeros_like(acc_ref)
    c_first = cq_ref[0, 0][:, 0:1]
    dh = q_ref.shape[1]

    def step(j, row_blocks):
        ks = pl.multiple_of(j * tk, tk)
        k = k_ref[pl.ds(ks, tk), :]
        v = v_ref[pl.ds(ks, tk), :]
        v_aug = jnp.concatenate([v, jnp.ones_like(v)], axis=1)
        bias = c_first - ck_ref[0, j]
        for r, mask in row_blocks:
            rows = pl.ds(r * rb, rb)
            s = _dot(q_ref[rows, :], k, NT_DIMS) + bias
            if mask is not None:
                s = jnp.where(mask, s, NEG_BIG)
            m_prev = m_ref[rows, :]
            m_new = jnp.maximum(m_prev, jnp.max(s, axis=-1, keepdims=True))
            alpha = jnp.exp2(m_prev - m_new)
            pmat = jnp.exp2(s - jnp.concatenate([m_new] * (tk // dh), axis=1))
            acc_ref[rows, :] = (jnp.concatenate([alpha, alpha], axis=1) * acc_ref[rows, :]
                                + _dot(pmat.astype(BF16), v_aug))
            m_ref[rows, :] = m_new

    full = [(r, None) for r in range(n_rb)]

    def make_body(n_blocks, first):
        def body(jo, carry):
            for jj in range(n_blocks):
                step(first + jo * n_blocks + jj, full)
            return carry
        return body

    n_full = i * kpq
    n_big = n_full // FOX_BLOCKS_PER_TRIP
    lax.fori_loop(0, n_big, make_body(FOX_BLOCKS_PER_TRIP, 0), 0)
    n_done = n_big * FOX_BLOCKS_PER_TRIP
    lax.fori_loop(0, (n_full - n_done) // kpq, make_body(kpq, n_done), 0)
    for jj in range(kpq):
        blocks = []
        for r in range(n_rb):
            r0, r1 = r * rb, (r + 1) * rb - 1
            c0, c1 = jj * tk, (jj + 1) * tk - 1
            if r1 < c0:
                continue
            if r0 >= c1:
                blocks.append((r, None))
            else:
                row = r0 + lax.broadcasted_iota(jnp.int32, (rb, tk), 0)
                col = c0 + lax.broadcasted_iota(jnp.int32, (rb, tk), 1)
                blocks.append((r, col <= row))
        step(i * kpq + jj, blocks)
    o = acc_ref[:, :dh] / acc_ref[:, dh:]
    o_ref[...] = (o * g_ref[...].astype(F32)).astype(o_ref.dtype)


def _fox_attention(q, k, v, c, gate, *, tq=2048, tk=512, rb=256):
    m, d = q.shape
    nh = d // FOX_HEAD_DIM
    tq = min(tq, m)
    tk = min(tk, tq)
    rb = min(rb, tq)
    dh = FOX_HEAD_DIM
    cq = c.reshape(nh, m // tq, 1, tq)
    ck = c.reshape(nh, m // tk, 1, tk)
    return pl.pallas_call(
        functools.partial(_fox_attn_kernel, tq=tq, tk=tk, rb=rb),
        grid=(nh, m // tq),
        in_specs=[
            pl.BlockSpec((tq, dh), lambda h, i: (i, h)),
            pl.BlockSpec((m, dh), lambda h, i: (0, h)),
            pl.BlockSpec((m, dh), lambda h, i: (0, h)),
            pl.BlockSpec((1, 1, 1, tq), lambda h, i: (h, i, 0, 0)),
            pl.BlockSpec((1, m // tk, 1, tk), lambda h, i: (h, 0, 0, 0)),
            pl.BlockSpec((tq, dh), lambda h, i: (i, h)),
        ],
        out_specs=pl.BlockSpec((tq, dh), lambda h, i: (i, h)),
        out_shape=jax.ShapeDtypeStruct((m, d), BF16),
        scratch_shapes=[
            pltpu.VMEM((tq, dh), F32),
            pltpu.VMEM((tq, 2 * dh), F32),
        ],
        compiler_params=_cparams("parallel", "arbitrary"),
        name="fox_attn",
    )(q, k, v, cq, ck, gate)


def _fox_layer(x, norm_g, w_in, b_f, qk_gain, w_out):
    m, d = x.shape
    nh = d // FOX_HEAD_DIM
    w_in_b = w_in.astype(BF16)
    tn = 512
    scale = FOX_HEAD_DIM ** -0.5 * LOG2_E
    gq = jnp.tile(qk_gain[0] * scale, tn // FOX_HEAD_DIM).reshape(1, tn)
    gk = jnp.tile(qk_gain[1], tn // FOX_HEAD_DIM).reshape(1, tn)
    q, k, v, gate = _mm(
        x, [(w_in_b, 0), (w_in_b, d), (w_in_b, 2 * d), (w_in_b, 3 * d)],
        _fox_qkvg_epilogue, [BF16, BF16, BF16, BF16], n=d, rows=[gq, gk], gain=norm_g, tn=tn)
    wf_t = w_in[:, 4 * d:].T.astype(BF16)
    c = _fox_forget(x, norm_g, wf_t, b_f)
    og = _fox_attention(q, k, v, c, gate)
    (x_new,) = _mm(og, [(w_out.astype(BF16), 0)],
                   lambda accs, rows, tiles: (tiles[0] + accs[0],), [F32], n=d, tiles=[x])
    return x_new


def _rwkv_prep_kernel(x_ref, xp_ref, g_ref, mu_ref, *rest):
    out_refs = rest[:6]
    hbuf = rest[6]
    i = pl.program_id(0)
    tm = x_ref.shape[0]
    x = x_ref[...]
    g = g_ref[...]
    h = x * _rms_scale(x) * g
    xp = xp_ref[...][7:8, :]
    hp = xp * _rms_scale(xp) * g
    hp = jnp.where(i == 0, jnp.zeros_like(hp), hp)
    hbuf[pl.ds(8, tm), :] = h
    hbuf[pl.ds(7, 1), :] = hp
    xx = hbuf[pl.ds(7, tm), :] - h
    mu = mu_ref[...]
    for n, o_ref in enumerate(out_refs):
        o_ref[...] = (h + xx * mu[n:n + 1, :]).astype(o_ref.dtype)


def _rwkv_prep(x, g, mu, *, tm=256):
    m, d = x.shape
    tm = min(tm, m)
    rb = tm // 8
    return pl.pallas_call(
        _rwkv_prep_kernel,
        grid=(m // tm,),
        in_specs=[
            pl.BlockSpec((tm, d), lambda i: (i, 0)),
            pl.BlockSpec((8, d), lambda i: (jnp.maximum(i * rb - 1, 0), 0)),
            pl.BlockSpec((1, d), lambda i: (0, 0)),
            pl.BlockSpec((8, d), lambda i: (0, 0)),
        ],
        out_specs=[pl.BlockSpec((tm, d), lambda i: (i, 0)) for _ in range(6)],
        out_shape=[jax.ShapeDtypeStruct((m, d), BF16) for _ in range(6)],
        scratch_shapes=[pltpu.VMEM((tm + 8, d), F32)],
        compiler_params=_cparams("parallel"),
        name="rwkv_prep",
    )(x, x, g.reshape(1, d), jnp.pad(mu, ((0, 2), (0, 0))))


def _lora_kernel(x_ref, w1_ref, w2_ref, b_ref, o_ref, *, mid_act, out_act):
    t = _dot(x_ref[...], w1_ref[...])
    t = mid_act(t).astype(BF16)
    y = _dot(t, w2_ref[...]) + b_ref[...]
    o_ref[...] = out_act(y).astype(o_ref.dtype)


def _lora(x, w1, w2, bias, mid_act, out_act, *, tm=512):
    m, d = x.shape
    r = w1.shape[1]
    n = w2.shape[1]
    tm = min(tm, m)
    return pl.pallas_call(
        functools.partial(_lora_kernel, mid_act=mid_act, out_act=out_act),
        grid=(m // tm,),
        in_specs=[
            pl.BlockSpec((tm, d), lambda i: (i, 0)),
            pl.BlockSpec((d, r), lambda i: (0, 0)),
            pl.BlockSpec((r, n), lambda i: (0, 0)),
            pl.BlockSpec((1, n), lambda i: (0, 0)),
        ],
        out_specs=pl.BlockSpec((tm, n), lambda i: (i, 0)),
        out_shape=jax.ShapeDtypeStruct((m, n), F32),
        compiler_params=_cparams("parallel"),
        name="rwkv_lora",
    )(x, w1.astype(BF16), w2.astype(BF16), bias.reshape(1, n))


def _bdot(a, b, spec, passes=3):
    ein = lambda x, y: jnp.einsum(spec, x, y, preferred_element_type=F32)
    if passes == 1:
        return ein(a.astype(BF16), b.astype(BF16))
    a_hi, a_lo = _split_bf16(a, 2)
    b_hi, b_lo = _split_bf16(b, 2)
    return ein(a_hi, b_hi) + ein(a_hi, b_lo) + ein(a_lo, b_hi)


def _rwkv_core_kernel(r_ref, k_ref, v_ref, ld_ref, a_ref, g_ref, kk_ref, ka_ref, rk_ref,
                      lnw_ref, lnb_ref, tri_ref, o_ref, st_ref, *, chunk):
    t_idx = pl.program_id(1)

    @pl.when(t_idx == 0)
    def _():
        st_ref[...] = jnp.zeros_like(st_ref)

    tt = r_ref.shape[0]
    c = chunk
    nc = tt // c
    n = RWKV_HEAD_DIM
    shape3 = (nc, c, LANES)
    row_l = lax.broadcasted_iota(jnp.int32, (LANES, LANES), 0)
    col_l = lax.broadcasted_iota(jnp.int32, (LANES, LANES), 1)
    same_head = (row_l < n) == (col_l < n)
    seg_ones = same_head.astype(BF16)
    bd_mask = same_head.astype(F32)
    eye_l = (row_l == col_l).astype(F32)

    def seg_sum(x):
        acc = None
        for part in _split_bf16(x.reshape(tt, LANES), 2):
            t = _dot(part, seg_ones)
            acc = t if acc is None else acc + t
        return acc.reshape(shape3)

    r = r_ref[...].reshape(shape3)
    k = k_ref[...].reshape(shape3)
    v = v_ref[...].reshape(shape3)
    ld = ld_ref[...]
    a = a_ref[...].reshape(shape3)

    lane = lax.broadcasted_iota(jnp.int32, (1, 1, LANES), 2)
    head0 = lane < n
    m0 = head0.astype(F32)
    m1 = 1.0 - m0

    kk = k * kk_ref[...]
    kk = kk / jnp.maximum(jnp.sqrt(seg_sum(kk * kk)), RWKV_NORM_EPS)
    k2 = k * (1.0 + (a - 1.0) * ka_ref[...])
    av = -kk
    bv = kk * a
    gam = None
    ld = ld.reshape(shape3)
    tri = jnp.broadcast_to(tri_ref[...][None], (nc, c, c))
    for part in _split_bf16(ld, 3):
        t = jnp.einsum('bts,bsl->btl', tri, part, preferred_element_type=F32)
        gam = t if gam is None else gam + t
    g_end = gam[:, c - 1:c, :]
    at = av * jnp.exp(gam - ld)
    rt = r * jnp.exp(gam)
    e_neg = jnp.exp(-gam)
    bt = bv * e_neg
    kt = k2 * e_neg
    e_end = jnp.exp(g_end - gam)
    b_end = bv * e_end
    k_end = k2 * e_end

    both = lambda x: jnp.concatenate([x, x], axis=0)
    own = lambda x: jnp.where(head0, x[:nc], x[nc:])
    lhs = both(jnp.concatenate([at, rt], axis=1))
    gb = _bdot(lhs, jnp.concatenate([bt * m0, bt * m1], axis=0), 'bqd,bkd->bqk')
    gk = _bdot(lhs, jnp.concatenate([kt * m0, kt * m1], axis=0), 'bqd,bkd->bqk')
    t_row = lax.broadcasted_iota(jnp.int32, (1, c, c), 1)
    s_col = lax.broadcasted_iota(jnp.int32, (1, c, c), 2)
    strict = s_col < t_row
    incl = s_col <= t_row
    g_ab = jnp.where(strict, gb[:, :c], 0.0)
    g_rb = jnp.where(incl, gb[:, c:], 0.0)
    g_ak = jnp.where(strict, gk[:, :c], 0.0)
    g_rk = jnp.where(incl, gk[:, c:], 0.0)
    tinv = (s_col == t_row).astype(F32) + g_ab
    npow = g_ab
    for _ in range(int(math.log2(c)) - 1):
        npow = _bdot(npow, npow, 'bij,bjk->bik', passes=1)
        tinv = tinv + _bdot(tinv, npow, 'bij,bjk->bik', passes=1)
    apply = lambda w, x: _bdot(w, x, 'bts,bsl->btl', passes=1)
    outer = lambda x, y: _bdot(x, y, 'bti,btj->bij', passes=1)
    v2 = both(v)
    av_v = own(apply(g_ak, v2))
    pmat = own(apply(tinv, both(at)))
    qmat = own(apply(tinv, both(av_v)))
    m_mat = outer(b_end, pmat) * bd_mask + eye_l * jnp.exp(g_end)
    n_mat = (outer(b_end, qmat) + outer(k_end, v)) * bd_mask
    o1 = rt + own(apply(g_rb, both(pmat)))
    o2 = own(apply(g_rb, both(qmat)) + apply(g_rk, v2))

    def halves(x):
        x2 = x.reshape((x.shape[0] // 2, 2) + x.shape[1:])
        return x2[:, 0], x2[:, 1]

    def interleave(xa, xb):
        return jnp.stack([xa, xb], axis=1).reshape((2 * xa.shape[0],) + xa.shape[1:])

    levels = []
    m_cur, n_cur = m_mat, n_mat
    for _ in range(RWKV_SWEEP_LEVELS):
        (m_a, m_b), (n_a, n_b) = halves(m_cur), halves(n_cur)
        levels.append((m_a, n_a))
        m_cur, n_cur = apply(m_b, m_a), apply(m_b, n_a) + n_b
    st = st_ref[...]
    entering = []
    for p in range(m_cur.shape[0]):
        entering.append(st)
        st = _dot_f32(m_cur[p], st, passes=1) + n_cur[p]
    st_ref[...] = st
    st_all = jnp.stack(entering, axis=0)
    for m_a, n_a in reversed(levels):
        st_all = interleave(st_all, apply(m_a, st_all) + n_a)
    o = apply(o1, st_all) + o2

    inv_n = 1.0 / n
    mean = seg_sum(o) * inv_n
    cen = o - mean
    var = seg_sum(cen * cen) * inv_n
    y = cen * lax.rsqrt(var + RWKV_LN_EPS) * lnw_ref[...] + lnb_ref[...]
    bonus = seg_sum(r * k2 * rk_ref[...]) * v
    out = (y + bonus) * g_ref[...].reshape(shape3)
    o_ref[...] = out.reshape(tt, LANES).astype(o_ref.dtype)


def _rwkv_core(r, k, v, ld, a, g, k_k, k_a, r_k, ln_w, ln_b, *, tt=1024):
    m, d = r.shape
    tt = min(tt, m)
    chunk = min(RWKV_CHUNK, tt)
    assert m % tt == 0 and (tt // chunk) % (2 ** RWKV_SWEEP_LEVELS) == 0, (m, tt, chunk)
    tri = jnp.tril(jnp.ones((chunk, chunk), F32)).astype(BF16)
    seq = pl.BlockSpec((tt, LANES), lambda hp, t: (t, hp))
    par = pl.BlockSpec((1, LANES), lambda hp, t: (0, hp))
    return pl.pallas_call(
        functools.partial(_rwkv_core_kernel, chunk=chunk),
        grid=(d // LANES, m // tt),
        in_specs=[seq] * 6 + [par] * 5 + [pl.BlockSpec((chunk, chunk), lambda hp, t: (0, 0))],
        out_specs=seq,
        out_shape=jax.ShapeDtypeStruct((m, d), BF16),
        scratch_shapes=[pltpu.VMEM((LANES, LANES), F32)],
        compiler_params=_cparams("parallel", "arbitrary"),
        name="rwkv_core",
    )(r, k, v, ld, a, g, k_k.reshape(1, d), k_a.reshape(1, d), r_k.reshape(1, d),
      ln_w.reshape(1, d), ln_b.reshape(1, d), tri)


def _rwkv_layer(x, norm_g, mu, w_rkv, w0, w1, w2, a0, a1, a2, g1, g2, k_k, k_a, r_k, ln_w, ln_b, w_out):
    m, d = x.shape
    xr, xw, xk, xv, xa, xg = _rwkv_prep(x, norm_g, mu)
    w_rkv_b = w_rkv.astype(BF16)
    plain = lambda accs, rows, tiles: (accs[0],)
    (r,) = _mm(xr, [(w_rkv_b[0], 0)], plain, [F32], n=d)
    (k,) = _mm(xk, [(w_rkv_b[1], 0)], plain, [F32], n=d)
    (v,) = _mm(xv, [(w_rkv_b[2], 0)], plain, [F32], n=d)
    ident = lambda t: t
    ld = _lora(xw, w1, w2, w0, jnp.tanh, lambda y: -jnp.exp(-_softplus(-y) - 0.5))
    a = _lora(xa, a1, a2, a0, ident, jax.nn.sigmoid)
    g = _lora(xg, g1, g2, jnp.zeros((d,), F32), jax.nn.sigmoid, ident)
    y = _rwkv_core(r, k, v, ld, a, g, k_k, k_a, r_k, ln_w, ln_b)
    (x_new,) = _mm(y, [(w_out.astype(BF16), 0)],
                   lambda accs, rows, tiles: (tiles[0] + accs[0],), [F32], n=d, tiles=[x])
    return x_new


def _s5_core_kernel(u_ref, b_ref, tab_ref, c_ref, d_ref, o_ref, hbuf, cbuf, hb16, carry_ref):
    t_idx = pl.program_id(1)
    tl = u_ref.shape[0]
    width = hbuf.shape[1]
    half = width // 2
    rc = tab_ref.shape[2]
    n_lvl = len(S5_LOCAL_SHIFTS)
    re = pl.ds(0, half)
    im = pl.ds(half, half)

    @pl.when(t_idx == 0)
    def _():
        carry_ref[...] = jnp.zeros_like(carry_ref)

    u = u_ref[...]
    hbuf[...] = _dot(u.astype(BF16), b_ref[0])

    def local(n, carry):
        rows = pl.ds(pl.multiple_of(n * rc, rc), rc)
        xr = hbuf[rows, re]
        xi = hbuf[rows, im]
        for lvl, s in enumerate(S5_LOCAL_SHIFTS):
            tr = tab_ref[0, lvl, :, re]
            ti = tab_ref[0, lvl, :, im]
            sr = pltpu.roll(xr, s, axis=0)
            si = pltpu.roll(xi, s, axis=0)
            xr, xi = xr + tr * sr - ti * si, xi + tr * si + ti * sr
        hbuf[rows, re] = xr
        hbuf[rows, im] = xi
        return carry

    lax.fori_loop(0, tl // rc, local, 0)

    a8r = tab_ref[0, n_lvl, pl.ds(7, 1), re]
    a8i = tab_ref[0, n_lvl, pl.ds(7, 1), im]
    cr = carry_ref[0:1, re]
    ci = carry_ref[0:1, im]
    for g in range(tl // 8):
        cbuf[pl.ds(8 * g, 8), re] = jnp.broadcast_to(cr, (8, half))
        cbuf[pl.ds(8 * g, 8), im] = jnp.broadcast_to(ci, (8, half))
        lr = hbuf[pl.ds(8 * g + 7, 1), re]
        li = hbuf[pl.ds(8 * g + 7, 1), im]
        cr, ci = lr + a8r * cr - a8i * ci, li + a8r * ci + a8i * cr
    carry_ref[0:1, re] = cr
    carry_ref[0:1, im] = ci

    def apply(n, carry):
        rows = pl.ds(pl.multiple_of(n * rc, rc), rc)
        pr = tab_ref[0, n_lvl, :, re]
        pi = tab_ref[0, n_lvl, :, im]
        br = cbuf[rows, re]
        bi = cbuf[rows, im]
        hb16[rows, re] = (hbuf[rows, re] + pr * br - pi * bi).astype(BF16)
        hb16[rows, im] = (hbuf[rows, im] + pr * bi + pi * br).astype(BF16)
        return carry

    lax.fori_loop(0, tl // rc, apply, 0)
    y = _dot(hb16[...], c_ref[0]) + d_ref[...] * u
    o_ref[...] = jax.nn.gelu(y).astype(o_ref.dtype)


def _s5_core(u, b_blk, tab, c_blk, d_skip, *, tl=1024):
    m, d = u.shape
    tl = min(tl, m)
    nb = d // LANES
    width = b_blk.shape[2]
    n_tab, rc = tab.shape[1], tab.shape[2]
    return pl.pallas_call(
        _s5_core_kernel,
        grid=(nb, m // tl),
        in_specs=[
            pl.BlockSpec((tl, LANES), lambda gb, t: (t, gb)),
            pl.BlockSpec((1, LANES, width), lambda gb, t: (gb, 0, 0)),
            pl.BlockSpec((1, n_tab, rc, width), lambda gb, t: (gb, 0, 0, 0)),
            pl.BlockSpec((1, width, LANES), lambda gb, t: (gb, 0, 0)),
            pl.BlockSpec((1, LANES), lambda gb, t: (0, gb)),
        ],
        out_specs=pl.BlockSpec((tl, LANES), lambda gb, t: (t, gb)),
        out_shape=jax.ShapeDtypeStruct((m, d), BF16),
        scratch_shapes=[pltpu.VMEM((tl, width), F32), pltpu.VMEM((tl, width), F32),
                        pltpu.VMEM((tl, width), BF16), pltpu.VMEM((8, width), F32)],
        compiler_params=_cparams("parallel", "arbitrary"),
        name="s5_core",
    )(u, b_blk, tab, c_blk, d_skip.reshape(1, d))


def _s5_tables(lam_re, lam_im, log_step, b_re, b_im, c_re, c_im):
    g, p = lam_re.shape
    q = b_re.shape[2]
    gpb = S5_GROUPS_PER_BLOCK
    nb = g // gpb
    lr = jnp.minimum(lam_re.astype(F32), S5_MAX_RE)
    li = lam_im.astype(F32)
    dt = jnp.exp(log_step.astype(F32))[:, None]
    mag = jnp.exp(lr * dt)
    abar_re, abar_im = mag * jnp.cos(li * dt), mag * jnp.sin(li * dt)
    den = lr * lr + li * li
    nr, ni = abar_re - 1.0, abar_im
    q_re, q_im = (nr * lr + ni * li) / den, (ni * lr - nr * li) / den
    br, bi = b_re.astype(F32), b_im.astype(F32)
    bbar_re = q_re[..., None] * br - q_im[..., None] * bi
    bbar_im = q_re[..., None] * bi + q_im[..., None] * br
    eye = jnp.eye(gpb, dtype=F32)

    def blockdiag_in(bb):
        t = bb.reshape(nb, gpb, p, q).transpose(0, 1, 3, 2)
        return jnp.einsum('ngqp,gh->ngqhp', t, eye).reshape(nb, gpb * q, gpb * p)

    def blockdiag_out(cc):
        t = cc.reshape(nb, gpb, q, p).transpose(0, 1, 3, 2)
        return jnp.einsum('ngpq,gh->ngphq', t, eye).reshape(nb, gpb * p, gpb * q)

    b_blk = jnp.concatenate([blockdiag_in(bbar_re), blockdiag_in(bbar_im)], axis=2).astype(BF16)
    c_blk = jnp.concatenate([blockdiag_out(c_re.astype(F32)), -blockdiag_out(c_im.astype(F32))],
                            axis=1).astype(BF16)
    ar = abar_re.reshape(nb, gpb * p)
    ai = abar_im.reshape(nb, gpb * p)
    pows = [(ar, ai)]
    for _ in range(7):
        pr, pi = pows[-1]
        pows.append((pr * ar - pi * ai, pr * ai + pi * ar))
    row = jnp.arange(S5_SCAN_ROWS) % 8

    def table(vals):
        zero = jnp.zeros_like(ar)
        re = jnp.stack([vals[r][0] if vals[r] is not None else zero for r in range(8)], axis=1)
        im = jnp.stack([vals[r][1] if vals[r] is not None else zero for r in range(8)], axis=1)
        return jnp.concatenate([re[:, row], im[:, row]], axis=2)

    tabs = [table([pows[s - 1] if r >= s else None for r in range(8)]) for s in S5_LOCAL_SHIFTS]
    tabs.append(table([pows[r] for r in range(8)]))
    return b_blk, jnp.stack(tabs, axis=1), c_blk


def _s5_layer(x, norm_g, w_in, lam_re, lam_im, log_step, b_re, b_im, c_re, c_im, d_skip, w_out):
    m, d = x.shape
    (u,) = _mm(x, [(w_in.astype(BF16), 0)], lambda accs, rows, tiles: (accs[0],), [F32],
               n=d, gain=norm_g)
    b_blk, a_blk, c_blk = _s5_tables(lam_re, lam_im, log_step, b_re, b_im, c_re, c_im)
    y = _s5_core(u, b_blk, a_blk, c_blk, d_skip)
    w_out_b = w_out.astype(BF16)
    (x_new,) = _mm(y, [(w_out_b, 0), (w_out_b, d)],
                   lambda accs, rows, tiles: (tiles[0] + accs[0] * jax.nn.sigmoid(accs[1]),),
                   [F32], n=d, tiles=[x])
    return x_new


def kernel(x, norm_w, ffn_w_up, ffn_w_down, fox_w_in, fox_b_f, fox_qk_gain, fox_w_out, rwkv_mu, rwkv_w_rkv, rwkv_w0, rwkv_w1, rwkv_w2, rwkv_a0, rwkv_a1, rwkv_a2, rwkv_g1, rwkv_g2, rwkv_k_k, rwkv_k_a, rwkv_r_k, rwkv_ln_w, rwkv_ln_b, rwkv_w_out, s5_w_in, s5_lam_re, s5_lam_im, s5_log_step, s5_b_re, s5_b_im, s5_c_re, s5_c_im, s5_d, s5_w_out, final_norm):
    bsz, s, d = x.shape
    depth = norm_w.shape[0]
    outs = []
    for b in range(bsz):
        xb = x[b]
        ia = ib = ic = 0
        for i in range(depth):
            xb = _ffn(xb, norm_w[i, 0], ffn_w_up[i, 0], ffn_w_down[i, 0])
            mixer = i % 3
            if mixer == 0:
                xb = _fox_layer(xb, norm_w[i, 1], fox_w_in[ia], fox_b_f[ia], fox_qk_gain[ia], fox_w_out[ia])
                ia += 1
            elif mixer == 1:
                xb = _rwkv_layer(xb, norm_w[i, 1], rwkv_mu[ib], rwkv_w_rkv[ib], rwkv_w0[ib], rwkv_w1[ib],
                                 rwkv_w2[ib], rwkv_a0[ib], rwkv_a1[ib], rwkv_a2[ib], rwkv_g1[ib], rwkv_g2[ib],
                                 rwkv_k_k[ib], rwkv_k_a[ib], rwkv_r_k[ib].reshape(-1), rwkv_ln_w[ib],
                                 rwkv_ln_b[ib], rwkv_w_out[ib])
                ib += 1
            else:
                xb = _s5_layer(xb, norm_w[i, 1], s5_w_in[ic], s5_lam_re[ic], s5_lam_im[ic], s5_log_step[ic],
                               s5_b_re[ic], s5_b_im[ic], s5_c_re[ic], s5_c_im[ic], s5_d[ic], s5_w_out[ic])
                ic += 1
            fin = final_norm if i == depth - 1 else None
            xb = _ffn(xb, norm_w[i, 2], ffn_w_up[i, 1], ffn_w_down[i, 1], fin)
        outs.append(xb)
    return jnp.stack(outs, axis=0)
```

```python
import functools
import math

import jax
import jax.numpy as jnp
from jax import lax
from jax.experimental import pallas as pl
from jax.experimental.pallas import tpu as pltpu

F32 = jnp.float32
BF16 = jnp.bfloat16

V7X_VMEM_BYTES = 64 * 1024 * 1024
VMEM_LIMIT_BYTES = V7X_VMEM_BYTES - 8 * 1024 * 1024
LANES = 128

RMS_EPS = 1e-6
FOX_HEAD_DIM = 128
FOX_BLOCKS_PER_TRIP = 4
RWKV_HEAD_DIM = 64
RWKV_LN_EPS = 64e-5
RWKV_NORM_EPS = 1e-12
RWKV_CHUNK = 64
RWKV_SWEEP_LEVELS = 2
S5_GROUP = 16
S5_STATE = 64
S5_MAX_RE = -1e-4
S5_GROUPS_PER_BLOCK = LANES // S5_GROUP
NEG_BIG = -1e30
LOG2_E = math.log2(math.e)
S5_SCAN_ROWS = 32
S5_LOCAL_SHIFTS = (1, 2, 4)

NT_DIMS = (((1,), (1,)), ((), ()))
TN_DIMS = (((0,), (0,)), ((), ()))
NN_DIMS = (((1,), (0,)), ((), ()))


def _cparams(*sem, vmem_limit_bytes=VMEM_LIMIT_BYTES):
    return pltpu.CompilerParams(dimension_semantics=sem, vmem_limit_bytes=vmem_limit_bytes)


def _rms_scale(x):
    return lax.rsqrt(jnp.mean(x * x, axis=-1, keepdims=True) + RMS_EPS)


def _softplus(z):
    return jnp.maximum(z, 0.0) + jnp.log1p(jnp.exp(-jnp.abs(z)))


def _dot(a, b, dims=NN_DIMS):
    return lax.dot_general(a, b, dims, preferred_element_type=F32)


def _split_bf16(x, parts):
    out = []
    rem = x
    for _ in range(parts):
        p = rem.astype(BF16)
        out.append(p)
        rem = rem - p.astype(F32)
    return out


def _dot_f32(a, b, dims=NN_DIMS, passes=3):
    if passes == 1:
        return _dot(a.astype(BF16), b.astype(BF16), dims)
    a_p = _split_bf16(a, 2 if passes == 3 else 3)
    b_p = _split_bf16(b, 2 if passes == 3 else 3)
    acc = None
    for ia, ap in enumerate(a_p):
        for ib, bp in enumerate(b_p):
            if ia + ib >= len(a_p):
                continue
            t = _dot(ap, bp, dims)
            acc = t if acc is None else acc + t
    return acc


def _ffn_kernel(x_ref, g_ref, wg_ref, wu_ref, wd_ref, *rest, final_norm):
    if final_norm:
        fin_ref, o_ref, h_ref = rest
    else:
        o_ref, h_ref = rest
    j = pl.program_id(1)

    @pl.when(j == 0)
    def _():
        x = x_ref[...]
        h_ref[...] = (x * _rms_scale(x) * g_ref[...]).astype(BF16)
        o_ref[...] = jnp.zeros_like(o_ref)

    h = h_ref[...]
    gate = _dot(h, wg_ref[...])
    up = _dot(h, wu_ref[...])
    act = (gate * jax.nn.sigmoid(gate) * up).astype(BF16)
    o_ref[...] += _dot(act, wd_ref[...])

    @pl.when(j == pl.num_programs(1) - 1)
    def _():
        y = x_ref[...] + 0.5 * o_ref[...]
        if final_norm:
            y = y * _rms_scale(y) * fin_ref[...]
        o_ref[...] = y


def _ffn(x, g, w_up, w_down, fin=None, *, tm=512, tf=512):
    m, d = x.shape
    f = w_down.shape[0]
    tm = min(tm, m)
    nf = f // tf
    w_up_b = w_up.astype(BF16)
    in_specs = [
        pl.BlockSpec((tm, d), lambda i, j: (i, 0)),
        pl.BlockSpec((1, d), lambda i, j: (0, 0)),
        pl.BlockSpec((d, tf), lambda i, j: (0, j)),
        pl.BlockSpec((d, tf), lambda i, j: (0, j + nf)),
        pl.BlockSpec((tf, d), lambda i, j: (j, 0)),
    ]
    args = [x, g.reshape(1, d), w_up_b, w_up_b, w_down.astype(BF16)]
    if fin is not None:
        in_specs.append(pl.BlockSpec((1, d), lambda i, j: (0, 0)))
        args.append(fin.reshape(1, d))
    return pl.pallas_call(
        functools.partial(_ffn_kernel, final_norm=fin is not None),
        grid=(m // tm, nf),
        in_specs=in_specs,
        out_specs=pl.BlockSpec((tm, d), lambda i, j: (i, 0)),
        out_shape=jax.ShapeDtypeStruct((m, d), F32),
        scratch_shapes=[pltpu.VMEM((tm, d), BF16)],
        compiler_params=_cparams("parallel", "arbitrary"),
        name="ffn",
    )(*args)


def _mm_kernel(*refs, n_b, n_row, n_tile, n_out, rmsnorm, epilogue):
    pos = 0
    a_ref = refs[pos]; pos += 1
    if rmsnorm:
        g_ref = refs[pos]; pos += 1
    b_refs = refs[pos:pos + n_b]; pos += n_b
    row_refs = refs[pos:pos + n_row]; pos += n_row
    tile_refs = refs[pos:pos + n_tile]; pos += n_tile
    out_refs = refs[pos:pos + n_out]; pos += n_out
    if rmsnorm:
        h_ref = refs[pos]

        @pl.when(pl.program_id(1) == 0)
        def _():
            x = a_ref[...]
            h_ref[...] = (x * _rms_scale(x) * g_ref[...]).astype(BF16)

        a = h_ref[...]
    else:
        a = a_ref[...]
    accs = [_dot(a, b_ref[...]) for b_ref in b_refs]
    outs = epilogue(accs, [r[...] for r in row_refs], [t[...] for t in tile_refs])
    for o_ref, o in zip(out_refs, outs):
        o_ref[...] = o.astype(o_ref.dtype)


def _mm(a, bs, epilogue, out_dtypes, *, n, rows=(), tiles=(), gain=None, tm=512, tn=None):
    m, k = a.shape
    tm = min(tm, m)
    if tn is None:
        tn = max(n // len(bs), 4 * LANES)
    tn = min(tn, n)
    rmsnorm = gain is not None
    in_specs = [pl.BlockSpec((tm, k), lambda i, j: (i, 0))]
    args = [a]
    if rmsnorm:
        in_specs.append(pl.BlockSpec((1, k), lambda i, j: (0, 0)))
        args.append(gain.reshape(1, k))
    for b, off in bs:
        ob = off // tn
        in_specs.append(pl.BlockSpec((k, tn), lambda i, j, ob=ob: (0, j + ob)))
        args.append(b)
    for r in rows:
        if r.shape[1] == tn and n != tn:
            in_specs.append(pl.BlockSpec((1, tn), lambda i, j: (0, 0)))
        else:
            in_specs.append(pl.BlockSpec((1, tn), lambda i, j: (0, j)))
        args.append(r)
    for t in tiles:
        in_specs.append(pl.BlockSpec((tm, tn), lambda i, j: (i, j)))
        args.append(t)
    out_shape = [jax.ShapeDtypeStruct((m, n), dt) for dt in out_dtypes]
    out_specs = [pl.BlockSpec((tm, tn), lambda i, j: (i, j)) for _ in out_dtypes]
    kern = functools.partial(
        _mm_kernel, n_b=len(bs), n_row=len(rows), n_tile=len(tiles), n_out=len(out_dtypes),
        rmsnorm=rmsnorm, epilogue=epilogue)
    return pl.pallas_call(
        kern,
        grid=(m // tm, n // tn),
        in_specs=in_specs,
        out_specs=out_specs,
        out_shape=out_shape,
        scratch_shapes=[pltpu.VMEM((tm, k), BF16)] if rmsnorm else [],
        compiler_params=_cparams("parallel", "arbitrary"),
        name="proj",
    )(*args)


def _fox_qkvg_epilogue(accs, rows, tiles):
    gq, gk = rows
    q, k, v, g = accs
    tn = q.shape[1]

    def head_norm(t, gain):
        parts = []
        for hh in range(tn // FOX_HEAD_DIM):
            sl = t[:, hh * FOX_HEAD_DIM:(hh + 1) * FOX_HEAD_DIM]
            parts.append(sl * _rms_scale(sl))
        return jnp.concatenate(parts, axis=1) * gain

    return head_norm(q, gq), head_norm(k, gk), v, jax.nn.sigmoid(g)


def _fox_forget_kernel(x_ref, g_ref, wf_ref, bf_ref, tri_ref, c_ref, carry_ref):
    i = pl.program_id(0)

    @pl.when(i == 0)
    def _():
        carry_ref[...] = jnp.zeros_like(carry_ref)

    x = x_ref[...]
    h = (x * _rms_scale(x) * g_ref[...]).astype(BF16)
    f_logit = _dot(wf_ref[...], h, NT_DIMS) + bf_ref[...]
    log_f = -_softplus(-f_logit) * LOG2_E
    tri = tri_ref[...]
    local = None
    for part in _split_bf16(log_f, 3):
        t = _dot(part, tri)
        local = t if local is None else local + t
    c = local + carry_ref[...]
    c_ref[...] = c
    carry_ref[...] = c[:, -1:]


def _fox_forget(x, g, wf_t, b_f, *, tm=512):
    m, d = x.shape
    nh = wf_t.shape[0]
    tm = min(tm, m)
    tri = jnp.triu(jnp.ones((tm, tm), F32)).astype(BF16)
    return pl.pallas_call(
        _fox_forget_kernel,
        grid=(m // tm,),
        in_specs=[
            pl.BlockSpec((tm, d), lambda i: (i, 0)),
            pl.BlockSpec((1, d), lambda i: (0, 0)),
            pl.BlockSpec((nh, d), lambda i: (0, 0)),
            pl.BlockSpec((nh, 1), lambda i: (0, 0)),
            pl.BlockSpec((tm, tm), lambda i: (0, 0)),
        ],
        out_specs=pl.BlockSpec((nh, tm), lambda i: (0, i)),
        out_shape=jax.ShapeDtypeStruct((nh, m), F32),
        scratch_shapes=[pltpu.VMEM((nh, 1), F32)],
        compiler_params=_cparams("arbitrary"),
        name="fox_forget",
    )(x, g.reshape(1, d), wf_t, b_f.reshape(nh, 1), tri)


def _fox_attn_kernel(q_ref, k_ref, v_ref, cq_ref, ck_ref, g_ref, o_ref, m_ref, acc_ref,
                     *, tq, tk, rb):
    i = pl.program_id(1)
    n_rb = tq // rb
    kpq = tq // tk
    m_ref[...] = jnp.full_like(m_ref, NEG_BIG)
    acc_ref[...] = jnp.zeros_like(acc_ref)
    c_first = cq_ref[0, 0][:, 0:1]
    dh = q_ref.shape[1]

    def step(j, row_blocks):
        ks = pl.multiple_of(j * tk, tk)
        k = k_ref[pl.ds(ks, tk), :]
        v = v_ref[pl.ds(ks, tk), :]
        v_aug = jnp.concatenate([v, jnp.ones_like(v)], axis=1)
        bias = c_first - ck_ref[0, j]
        for r, mask in row_blocks:
            rows = pl.ds(r * rb, rb)
            s = _dot(q_ref[rows, :], k, NT_DIMS) + bias
            if mask is not None:
                s = jnp.where(mask, s, NEG_BIG)
            m_prev = m_ref[rows, :]
            m_new = jnp.maximum(m_prev, jnp.max(s, axis=-1, keepdims=True))
            alpha = jnp.exp2(m_prev - m_new)
            pmat = jnp.exp2(s - jnp.concatenate([m_new] * (tk // dh), axis=1))
            acc_ref[rows, :] = (jnp.concatenate([alpha, alpha], axis=1) * acc_ref[rows, :]
                                + _dot(pmat.astype(BF16), v_aug))
            m_ref[rows, :] = m_new

    full = [(r, None) for r in range(n_rb)]

    def make_body(n_blocks, first):
        def body(jo, carry):
            for jj in range(n_blocks):
                step(first + jo * n_blocks + jj, full)
            return carry
        return body

    n_full = i * kpq
    n_big = n_full // FOX_BLOCKS_PER_TRIP
    lax.fori_loop(0, n_big, make_body(FOX_BLOCKS_PER_TRIP, 0), 0)
    n_done = n_big * FOX_BLOCKS_PER_TRIP
    lax.fori_loop(0, (n_full - n_done) // kpq, make_body(kpq, n_done), 0)
    for jj in range(kpq):
        blocks = []
        for r in range(n_rb):
            r0, r1 = r * rb, (r + 1) * rb - 1
            c0, c1 = jj * tk, (jj + 1) * tk - 1
            if r1 < c0:
                continue
            if r0 >= c1:
                blocks.append((r, None))
            else:
                row = r0 + lax.broadcasted_iota(jnp.int32, (rb, tk), 0)
                col = c0 + lax.broadcasted_iota(jnp.int32, (rb, tk), 1)
                blocks.append((r, col <= row))
        step(i * kpq + jj, blocks)
    o = acc_ref[:, :dh] / acc_ref[:, dh:]
    o_ref[...] = (o * g_ref[...].astype(F32)).astype(o_ref.dtype)


def _fox_attention(q, k, v, c, gate, *, tq=2048, tk=512, rb=256):
    m, d = q.shape
    nh = d // FOX_HEAD_DIM
    tq = min(tq, m)
    tk = min(tk, tq)
    rb = min(rb, tq)
    dh = FOX_HEAD_DIM
    cq = c.reshape(nh, m // tq, 1, tq)
    ck = c.reshape(nh, m // tk, 1, tk)
    return pl.pallas_call(
        functools.partial(_fox_attn_kernel, tq=tq, tk=tk, rb=rb),
        grid=(nh, m // tq),
        in_specs=[
            pl.BlockSpec((tq, dh), lambda h, i: (i, h)),
            pl.BlockSpec((m, dh), lambda h, i: (0, h)),
            pl.BlockSpec((m, dh), lambda h, i: (0, h)),
            pl.BlockSpec((1, 1, 1, tq), lambda h, i: (h, i, 0, 0)),
            pl.BlockSpec((1, m // tk, 1, tk), lambda h, i: (h, 0, 0, 0)),
            pl.BlockSpec((tq, dh), lambda h, i: (i, h)),
        ],
        out_specs=pl.BlockSpec((tq, dh), lambda h, i: (i, h)),
        out_shape=jax.ShapeDtypeStruct((m, d), BF16),
        scratch_shapes=[
            pltpu.VMEM((tq, dh), F32),
            pltpu.VMEM((tq, 2 * dh), F32),
        ],
        compiler_params=_cparams("parallel", "arbitrary"),
        name="fox_attn",
    )(q, k, v, cq, ck, gate)


def _fox_layer(x, norm_g, w_in, b_f, qk_gain, w_out):
    m, d = x.shape
    nh = d // FOX_HEAD_DIM
    w_in_b = w_in.astype(BF16)
    tn = 512
    scale = FOX_HEAD_DIM ** -0.5 * LOG2_E
    gq = jnp.tile(qk_gain[0] * scale, tn // FOX_HEAD_DIM).reshape(1, tn)
    gk = jnp.tile(qk_gain[1], tn // FOX_HEAD_DIM).reshape(1, tn)
    q, k, v, gate = _mm(
        x, [(w_in_b, 0), (w_in_b, d), (w_in_b, 2 * d), (w_in_b, 3 * d)],
        _fox_qkvg_epilogue, [BF16, BF16, BF16, BF16], n=d, rows=[gq, gk], gain=norm_g, tn=tn)
    wf_t = w_in[:, 4 * d:].T.astype(BF16)
    c = _fox_forget(x, norm_g, wf_t, b_f)
    og = _fox_attention(q, k, v, c, gate)
    (x_new,) = _mm(og, [(w_out.astype(BF16), 0)],
                   lambda accs, rows, tiles: (tiles[0] + accs[0],), [F32], n=d, tiles=[x])
    return x_new


def _rwkv_prep_kernel(x_ref, xp_ref, g_ref, mu_ref, *rest):
    out_refs = rest[:6]
    hbuf = rest[6]
    i = pl.program_id(0)
    tm = x_ref.shape[0]
    x = x_ref[...]
    g = g_ref[...]
    h = x * _rms_scale(x) * g
    xp = xp_ref[...][7:8, :]
    hp = xp * _rms_scale(xp) * g
    hp = jnp.where(i == 0, jnp.zeros_like(hp), hp)
    hbuf[pl.ds(8, tm), :] = h
    hbuf[pl.ds(7, 1), :] = hp
    xx = hbuf[pl.ds(7, tm), :] - h
    mu = mu_ref[...]
    for n, o_ref in enumerate(out_refs):
        o_ref[...] = (h + xx * mu[n:n + 1, :]).astype(o_ref.dtype)


def _rwkv_prep(x, g, mu, *, tm=256):
    m, d = x.shape
    tm = min(tm, m)
    rb = tm // 8
    return pl.pallas_call(
        _rwkv_prep_kernel,
        grid=(m // tm,),
        in_specs=[
            pl.BlockSpec((tm, d), lambda i: (i, 0)),
            pl.BlockSpec((8, d), lambda i: (jnp.maximum(i * rb - 1, 0), 0)),
            pl.BlockSpec((1, d), lambda i: (0, 0)),
            pl.BlockSpec((8, d), lambda i: (0, 0)),
        ],
        out_specs=[pl.BlockSpec((tm, d), lambda i: (i, 0)) for _ in range(6)],
        out_shape=[jax.ShapeDtypeStruct((m, d), BF16) for _ in range(6)],
        scratch_shapes=[pltpu.VMEM((tm + 8, d), F32)],
        compiler_params=_cparams("parallel"),
        name="rwkv_prep",
    )(x, x, g.reshape(1, d), jnp.pad(mu, ((0, 2), (0, 0))))


def _lora_kernel(x_ref, w1_ref, w2_ref, b_ref, o_ref, *, mid_act, out_act):
    t = _dot(x_ref[...], w1_ref[...])
    t = mid_act(t).astype(BF16)
    y = _dot(t, w2_ref[...]) + b_ref[...]
    o_ref[...] = out_act(y).astype(o_ref.dtype)


def _lora(x, w1, w2, bias, mid_act, out_act, *, tm=512):
    m, d = x.shape
    r = w1.shape[1]
    n = w2.shape[1]
    tm = min(tm, m)
    return pl.pallas_call(
        functools.partial(_lora_kernel, mid_act=mid_act, out_act=out_act),
        grid=(m // tm,),
        in_specs=[
            pl.BlockSpec((tm, d), lambda i: (i, 0)),
            pl.BlockSpec((d, r), lambda i: (0, 0)),
            pl.BlockSpec((r, n), lambda i: (0, 0)),
            pl.BlockSpec((1, n), lambda i: (0, 0)),
        ],
        out_specs=pl.BlockSpec((tm, n), lambda i: (i, 0)),
        out_shape=jax.ShapeDtypeStruct((m, n), F32),
        compiler_params=_cparams("parallel"),
        name="rwkv_lora",
    )(x, w1.astype(BF16), w2.astype(BF16), bias.reshape(1, n))


def _bdot(a, b, spec, passes=3):
    ein = lambda x, y: jnp.einsum(spec, x, y, preferred_element_type=F32)
    if passes == 1:
        return ein(a.astype(BF16), b.astype(BF16))
    a_hi, a_lo = _split_bf16(a, 2)
    b_hi, b_lo = _split_bf16(b, 2)
    return ein(a_hi, b_hi) + ein(a_hi, b_lo) + ein(a_lo, b_hi)


def _rwkv_core_kernel(r_ref, k_ref, v_ref, ld_ref, a_ref, g_ref, kk_ref, ka_ref, rk_ref,
                      lnw_ref, lnb_ref, tri_ref, o_ref, st_ref, *, chunk):
    t_idx = pl.program_id(1)

    @pl.when(t_idx == 0)
    def _():
        st_ref[...] = jnp.zeros_like(st_ref)

    tt = r_ref.shape[0]
    c = chunk
    nc = tt // c
    n = RWKV_HEAD_DIM
    shape3 = (nc, c, LANES)
    row_l = lax.broadcasted_iota(jnp.int32, (LANES, LANES), 0)
    col_l = lax.broadcasted_iota(jnp.int32, (LANES, LANES), 1)
    same_head = (row_l < n) == (col_l < n)
    seg_ones = same_head.astype(BF16)
    bd_mask = same_head.astype(F32)
    eye_l = (row_l == col_l).astype(F32)

    def seg_sum(x):
        acc = None
        for part in _split_bf16(x.reshape(tt, LANES), 2):
            t = _dot(part, seg_ones)
            acc = t if acc is None else acc + t
        return acc.reshape(shape3)

    r = r_ref[...].reshape(shape3)
    k = k_ref[...].reshape(shape3)
    v = v_ref[...].reshape(shape3)
    ld = ld_ref[...]
    a = a_ref[...].reshape(shape3)

    lane = lax.broadcasted_iota(jnp.int32, (1, 1, LANES), 2)
    head0 = lane < n
    m0 = head0.astype(F32)
    m1 = 1.0 - m0

    kk = k * kk_ref[...]
    kk = kk / jnp.maximum(jnp.sqrt(seg_sum(kk * kk)), RWKV_NORM_EPS)
    k2 = k * (1.0 + (a - 1.0) * ka_ref[...])
    av = -kk
    bv = kk * a
    gam = None
    ld = ld.reshape(shape3)
    tri = jnp.broadcast_to(tri_ref[...][None], (nc, c, c))
    for part in _split_bf16(ld, 3):
        t = jnp.einsum('bts,bsl->btl', tri, part, preferred_element_type=F32)
        gam = t if gam is None else gam + t
    g_end = gam[:, c - 1:c, :]
    at = av * jnp.exp(gam - ld)
    rt = r * jnp.exp(gam)
    e_neg = jnp.exp(-gam)
    bt = bv * e_neg
    kt = k2 * e_neg
    e_end = jnp.exp(g_end - gam)
    b_end = bv * e_end
    k_end = k2 * e_end

    both = lambda x: jnp.concatenate([x, x], axis=0)
    own = lambda x: jnp.where(head0, x[:nc], x[nc:])
    lhs = both(jnp.concatenate([at, rt], axis=1))
    gb = _bdot(lhs, jnp.concatenate([bt * m0, bt * m1], axis=0), 'bqd,bkd->bqk')
    gk = _bdot(lhs, jnp.concatenate([kt * m0, kt * m1], axis=0), 'bqd,bkd->bqk')
    t_row = lax.broadcasted_iota(jnp.int32, (1, c, c), 1)
    s_col = lax.broadcasted_iota(jnp.int32, (1, c, c), 2)
    strict = s_col < t_row
    incl = s_col <= t_row
    g_ab = jnp.where(strict, gb[:, :c], 0.0)
    g_rb = jnp.where(incl, gb[:, c:], 0.0)
    g_ak = jnp.where(strict, gk[:, :c], 0.0)
    g_rk = jnp.where(incl, gk[:, c:], 0.0)
    tinv = (s_col == t_row).astype(F32) + g_ab
    npow = g_ab
    for _ in range(int(math.log2(c)) - 1):
        npow = _bdot(npow, npow, 'bij,bjk->bik', passes=1)
        tinv = tinv + _bdot(tinv, npow, 'bij,bjk->bik', passes=1)
    apply = lambda w, x: _bdot(w, x, 'bts,bsl->btl', passes=1)
    outer = lambda x, y: _bdot(x, y, 'bti,btj->bij', passes=1)
    v2 = both(v)
    av_v = own(apply(g_ak, v2))
    pmat = own(apply(tinv, both(at)))
    qmat = own(apply(tinv, both(av_v)))
    m_mat = outer(b_end, pmat) * bd_mask + eye_l * jnp.exp(g_end)
    n_mat = (outer(b_end, qmat) + outer(k_end, v)) * bd_mask
    o1 = rt + own(apply(g_rb, both(pmat)))
    o2 = own(apply(g_rb, both(qmat)) + apply(g_rk, v2))

    def halves(x):
        x2 = x.reshape((x.shape[0] // 2, 2) + x.shape[1:])
        return x2[:, 0], x2[:, 1]

    def interleave(xa, xb):
        return jnp.stack([xa, xb], axis=1).reshape((2 * xa.shape[0],) + xa.shape[1:])

    levels = []
    m_cur, n_cur = m_mat, n_mat
    for _ in range(RWKV_SWEEP_LEVELS):
        (m_a, m_b), (n_a, n_b) = halves(m_cur), halves(n_cur)
        levels.append((m_a, n_a))
        m_cur, n_cur = apply(m_b, m_a), apply(m_b, n_a) + n_b
    st = st_ref[...]
    entering = []
    for p in range(m_cur.shape[0]):
        entering.append(st)
        st = _dot_f32(m_cur[p], st, passes=1) + n_cur[p]
    st_ref[...] = st
    st_all = jnp.stack(entering, axis=0)
    for m_a, n_a in reversed(levels):
        st_all = interleave(st_all, apply(m_a, st_all) + n_a)
    o = apply(o1, st_all) + o2

    inv_n = 1.0 / n
    mean = seg_sum(o) * inv_n
    cen = o - mean
    var = seg_sum(cen * cen) * inv_n
    y = cen * lax.rsqrt(var + RWKV_LN_EPS) * lnw_ref[...] + lnb_ref[...]
    bonus = seg_sum(r * k2 * rk_ref[...]) * v
    out = (y + bonus) * g_ref[...].reshape(shape3)
    o_ref[...] = out.reshape(tt, LANES).astype(o_ref.dtype)


def _rwkv_core(r, k, v, ld, a, g, k_k, k_a, r_k, ln_w, ln_b, *, tt=1024):
    m, d = r.shape
    tt = min(tt, m)
    chunk = min(RWKV_CHUNK, tt)
    assert m % tt == 0 and (tt // chunk) % (2 ** RWKV_SWEEP_LEVELS) == 0, (m, tt, chunk)
    tri = jnp.tril(jnp.ones((chunk, chunk), F32)).astype(BF16)
    seq = pl.BlockSpec((tt, LANES), lambda hp, t: (t, hp))
    par = pl.BlockSpec((1, LANES), lambda hp, t: (0, hp))
    return pl.pallas_call(
        functools.partial(_rwkv_core_kernel, chunk=chunk),
        grid=(d // LANES, m // tt),
        in_specs=[seq] * 6 + [par] * 5 + [pl.BlockSpec((chunk, chunk), lambda hp, t: (0, 0))],
        out_specs=seq,
        out_shape=jax.ShapeDtypeStruct((m, d), BF16),
        scratch_shapes=[pltpu.VMEM((LANES, LANES), F32)],
        compiler_params=_cparams("parallel", "arbitrary"),
        name="rwkv_core",
    )(r, k, v, ld, a, g, k_k.reshape(1, d), k_a.reshape(1, d), r_k.reshape(1, d),
      ln_w.reshape(1, d), ln_b.reshape(1, d), tri)


def _rwkv_layer(x, norm_g, mu, w_rkv, w0, w1, w2, a0, a1, a2, g1, g2, k_k, k_a, r_k, ln_w, ln_b, w_out):
    m, d = x.shape
    xr, xw, xk, xv, xa, xg = _rwkv_prep(x, norm_g, mu)
    w_rkv_b = w_rkv.astype(BF16)
    plain = lambda accs, rows, tiles: (accs[0],)
    (r,) = _mm(xr, [(w_rkv_b[0], 0)], plain, [F32], n=d)
    (k,) = _mm(xk, [(w_rkv_b[1], 0)], plain, [F32], n=d)
    (v,) = _mm(xv, [(w_rkv_b[2], 0)], plain, [F32], n=d)
    ident = lambda t: t
    ld = _lora(xw, w1, w2, w0, jnp.tanh, lambda y: -jnp.exp(-_softplus(-y) - 0.5))
    a = _lora(xa, a1, a2, a0, ident, jax.nn.sigmoid)
    g = _lora(xg, g1, g2, jnp.zeros((d,), F32), jax.nn.sigmoid, ident)
    y = _rwkv_core(r, k, v, ld, a, g, k_k, k_a, r_k, ln_w, ln_b)
    (x_new,) = _mm(y, [(w_out.astype(BF16), 0)],
                   lambda accs, rows, tiles: (tiles[0] + accs[0],), [F32], n=d, tiles=[x])
    return x_new


def _s5_core_kernel(u_ref, b_ref, tab_ref, c_ref, d_ref, o_ref, hbuf, cbuf, hb16, carry_ref):
    t_idx = pl.program_id(1)
    tl = u_ref.shape[0]
    width = hbuf.shape[1]
    half = width // 2
    rc = tab_ref.shape[2]
    n_lvl = len(S5_LOCAL_SHIFTS)
    re = pl.ds(0, half)
    im = pl.ds(half, half)

    @pl.when(t_idx == 0)
    def _():
        carry_ref[...] = jnp.zeros_like(carry_ref)

    u = u_ref[...]
    hbuf[...] = _dot(u.astype(BF16), b_ref[0])

    def local(n, carry):
        rows = pl.ds(pl.multiple_of(n * rc, rc), rc)
        xr = hbuf[rows, re]
        xi = hbuf[rows, im]
        for lvl, s in enumerate(S5_LOCAL_SHIFTS):
            tr = tab_ref[0, lvl, :, re]
            ti = tab_ref[0, lvl, :, im]
            sr = pltpu.roll(xr, s, axis=0)
            si = pltpu.roll(xi, s, axis=0)
            xr, xi = xr + tr * sr - ti * si, xi + tr * si + ti * sr
        hbuf[rows, re] = xr
        hbuf[rows, im] = xi
        return carry

    lax.fori_loop(0, tl // rc, local, 0)

    a8r = tab_ref[0, n_lvl, pl.ds(7, 1), re]
    a8i = tab_ref[0, n_lvl, pl.ds(7, 1), im]
    cr = carry_ref[0:1, re]
    ci = carry_ref[0:1, im]
    for g in range(tl // 8):
        cbuf[pl.ds(8 * g, 8), re] = jnp.broadcast_to(cr, (8, half))
        cbuf[pl.ds(8 * g, 8), im] = jnp.broadcast_to(ci, (8, half))
        lr = hbuf[pl.ds(8 * g + 7, 1), re]
        li = hbuf[pl.ds(8 * g + 7, 1), im]
        cr, ci = lr + a8r * cr - a8i * ci, li + a8r * ci + a8i * cr
    carry_ref[0:1, re] = cr
    carry_ref[0:1, im] = ci

    def apply(n, carry):
        rows = pl.ds(pl.multiple_of(n * rc, rc), rc)
        pr = tab_ref[0, n_lvl, :, re]
        pi = tab_ref[0, n_lvl, :, im]
        br = cbuf[rows, re]
        bi = cbuf[rows, im]
        hb16[rows, re] = (hbuf[rows, re] + pr * br - pi * bi).astype(BF16)
        hb16[rows, im] = (hbuf[rows, im] + pr * bi + pi * br).astype(BF16)
        return carry

    lax.fori_loop(0, tl // rc, apply, 0)
    y = _dot(hb16[...], c_ref[0]) + d_ref[...] * u
    o_ref[...] = jax.nn.gelu(y).astype(o_ref.dtype)


def _s5_core(u, b_blk, tab, c_blk, d_skip, *, tl=1024):
    m, d = u.shape
    tl = min(tl, m)
    nb = d // LANES
    width = b_blk.shape[2]
    n_tab, rc = tab.shape[1], tab.shape[2]
    return pl.pallas_call(
        _s5_core_kernel,
        grid=(nb, m // tl),
        in_specs=[
            pl.BlockSpec((tl, LANES), lambda gb, t: (t, gb)),
            pl.BlockSpec((1, LANES, width), lambda gb, t: (gb, 0, 0)),
            pl.BlockSpec((1, n_tab, rc, width), lambda gb, t: (gb, 0, 0, 0)),
            pl.BlockSpec((1, width, LANES), lambda gb, t: (gb, 0, 0)),
            pl.BlockSpec((1, LANES), lambda gb, t: (0, gb)),
        ],
        out_specs=pl.BlockSpec((tl, LANES), lambda gb, t: (t, gb)),
        out_shape=jax.ShapeDtypeStruct((m, d), BF16),
        scratch_shapes=[pltpu.VMEM((tl, width), F32), pltpu.VMEM((tl, width), F32),
                        pltpu.VMEM((tl, width), BF16), pltpu.VMEM((8, width), F32)],
        compiler_params=_cparams("parallel", "arbitrary"),
        name="s5_core",
    )(u, b_blk, tab, c_blk, d_skip.reshape(1, d))


def _s5_tables(lam_re, lam_im, log_step, b_re, b_im, c_re, c_im):
    g, p = lam_re.shape
    q = b_re.shape[2]
    gpb = S5_GROUPS_PER_BLOCK
    nb = g // gpb
    lr = jnp.minimum(lam_re.astype(F32), S5_MAX_RE)
    li = lam_im.astype(F32)
    dt = jnp.exp(log_step.astype(F32))[:, None]
    mag = jnp.exp(lr * dt)
    abar_re, abar_im = mag * jnp.cos(li * dt), mag * jnp.sin(li * dt)
    den = lr * lr + li * li
    nr, ni = abar_re - 1.0, abar_im
    q_re, q_im = (nr * lr + ni * li) / den, (ni * lr - nr * li) / den
    br, bi = b_re.astype(F32), b_im.astype(F32)
    bbar_re = q_re[..., None] * br - q_im[..., None] * bi
    bbar_im = q_re[..., None] * bi + q_im[..., None] * br
    eye = jnp.eye(gpb, dtype=F32)

    def blockdiag_in(bb):
        t = bb.reshape(nb, gpb, p, q).transpose(0, 1, 3, 2)
        return jnp.einsum('ngqp,gh->ngqhp', t, eye).reshape(nb, gpb * q, gpb * p)

    def blockdiag_out(cc):
        t = cc.reshape(nb, gpb, q, p).transpose(0, 1, 3, 2)
        return jnp.einsum('ngpq,gh->ngphq', t, eye).reshape(nb, gpb * p, gpb * q)

    b_blk = jnp.concatenate([blockdiag_in(bbar_re), blockdiag_in(bbar_im)], axis=2).astype(BF16)
    c_blk = jnp.concatenate([blockdiag_out(c_re.astype(F32)), -blockdiag_out(c_im.astype(F32))],
                            axis=1).astype(BF16)
    ar = abar_re.reshape(nb, gpb * p)
    ai = abar_im.reshape(nb, gpb * p)
    pows = [(ar, ai)]
    for _ in range(7):
        pr, pi = pows[-1]
        pows.append((pr * ar - pi * ai, pr * ai + pi * ar))
    row = jnp.arange(S5_SCAN_ROWS) % 8

    def table(vals):
        zero = jnp.zeros_like(ar)
        re = jnp.stack([vals[r][0] if vals[r] is not None else zero for r in range(8)], axis=1)
        im = jnp.stack([vals[r][1] if vals[r] is not None else zero for r in range(8)], axis=1)
        return jnp.concatenate([re[:, row], im[:, row]], axis=2)

    tabs = [table([pows[s - 1] if r >= s else None for r in range(8)]) for s in S5_LOCAL_SHIFTS]
    tabs.append(table([pows[r] for r in range(8)]))
    return b_blk, jnp.stack(tabs, axis=1), c_blk


def _s5_layer(x, norm_g, w_in, lam_re, lam_im, log_step, b_re, b_im, c_re, c_im, d_skip, w_out):
    m, d = x.shape
    (u,) = _mm(x, [(w_in.astype(BF16), 0)], lambda accs, rows, tiles: (accs[0],), [F32],
               n=d, gain=norm_g)
    b_blk, a_blk, c_blk = _s5_tables(lam_re, lam_im, log_step, b_re, b_im, c_re, c_im)
    y = _s5_core(u, b_blk, a_blk, c_blk, d_skip)
    w_out_b = w_out.astype(BF16)
    (x_new,) = _mm(y, [(w_out_b, 0), (w_out_b, d)],
                   lambda accs, rows, tiles: (tiles[0] + accs[0] * jax.nn.sigmoid(accs[1]),),
                   [F32], n=d, tiles=[x])
    return x_new


def kernel(x, norm_w, ffn_w_up, ffn_w_down, fox_w_in, fox_b_f, fox_qk_gain, fox_w_out, rwkv_mu, rwkv_w_rkv, rwkv_w0, rwkv_w1, rwkv_w2, rwkv_a0, rwkv_a1, rwkv_a2, rwkv_g1, rwkv_g2, rwkv_k_k, rwkv_k_a, rwkv_r_k, rwkv_ln_w, rwkv_ln_b, rwkv_w_out, s5_w_in, s5_lam_re, s5_lam_im, s5_log_step, s5_b_re, s5_b_im, s5_c_re, s5_c_im, s5_d, s5_w_out, final_norm):
    bsz, s, d = x.shape
    depth = norm_w.shape[0]
    outs = []
    for b in range(bsz):
        xb = x[b]
        ia = ib = ic = 0
        for i in range(depth):
            xb = _ffn(xb, norm_w[i, 0], ffn_w_up[i, 0], ffn_w_down[i, 0])
            mixer = i % 3
            if mixer == 0:
                xb = _fox_layer(xb, norm_w[i, 1], fox_w_in[ia], fox_b_f[ia], fox_qk_gain[ia], fox_w_out[ia])
                ia += 1
            elif mixer == 1:
                xb = _rwkv_layer(xb, norm_w[i, 1], rwkv_mu[ib], rwkv_w_rkv[ib], rwkv_w0[ib], rwkv_w1[ib],
                                 rwkv_w2[ib], rwkv_a0[ib], rwkv_a1[ib], rwkv_a2[ib], rwkv_g1[ib], rwkv_g2[ib],
                                 rwkv_k_k[ib], rwkv_k_a[ib], rwkv_r_k[ib].reshape(-1), rwkv_ln_w[ib],
                                 rwkv_ln_b[ib], rwkv_w_out[ib])
                ib += 1
            else:
                xb = _s5_layer(xb, norm_w[i, 1], s5_w_in[ic], s5_lam_re[ic], s5_lam_im[ic], s5_log_step[ic],
                               s5_b_re[ic], s5_b_im[ic], s5_c_re[ic], s5_c_im[ic], s5_d[ic], s5_w_out[ic])
                ic += 1
            fin = final_norm if i == depth - 1 else None
            xb = _ffn(xb, norm_w[i, 2], ffn_w_up[i, 1], ffn_w_down[i, 1], fin)
        outs.append(xb)
    return jnp.stack(outs, axis=0)
```
